```python
import math
import jax, jax.numpy as jnp
from jax import lax
import numpy as np

D_MODEL = 2048
BATCH = 8
SEQ = 2048
DEPTH = 2

BRANCH_WIDTH = D_MODEL
A_WIDTH = BRANCH_WIDTH
A_GROUPS = 8
A_CHUNK = 128
B_HEAD_DIM = 128
B_HEADS = BRANCH_WIDTH // B_HEAD_DIM
B_KV_GROUPS = 4
B_WIDTH = B_HEADS * B_HEAD_DIM
KV_WIDTH = B_KV_GROUPS * B_HEAD_DIM
CMP_BLOCK = 32
CMP_STRIDE = 16
SEL_BLOCK = 64
N_SELECT = 16
WINDOW = 512
NSA_QBLOCK = 64
FORCE_BONUS = 1e4
C_WIDTH = BRANCH_WIDTH
POOL_WINDOWS = (2, 4, 8, 16)
C_GROUP = C_WIDTH // len(POOL_WINDOWS)
REL_BUCKETS = 32
REL_MAX_DIST = 128
EPS = 1e-6
NEG_INF = -1e30

IN_SIZES = (A_WIDTH, A_WIDTH, A_WIDTH,
            B_WIDTH, 6 * KV_WIDTH, 3 * B_HEADS, B_WIDTH,
            C_WIDTH, C_WIDTH,
            3 * D_MODEL)
IN_SPLITS = tuple(int(v) for v in np.cumsum(IN_SIZES)[:-1])
N_IN_COLS = int(sum(IN_SIZES))

kernel_name = 'hybrid_gmlp_nsa_pool_adaln'


def rmsnorm(x, g):
    xf = x.astype(jnp.float32)
    y = xf * lax.rsqrt(jnp.mean(xf * xf, axis=-1, keepdims=True) + EPS)
    return (y * g).astype(x.dtype)


def layernorm(x, g, b):
    xf = x.astype(jnp.float32)
    mu = jnp.mean(xf, axis=-1, keepdims=True)
    var = jnp.mean((xf - mu) ** 2, axis=-1, keepdims=True)
    return ((xf - mu) * lax.rsqrt(var + EPS) * g + b).astype(x.dtype)


def masked_softmax(s, mask):
    s = jnp.where(mask, s.astype(jnp.float32), NEG_INF)
    return jnp.where(mask, jax.nn.softmax(s, axis=-1), 0.0)


def t5_bucket(n):
    max_exact = REL_BUCKETS // 2
    nf = jnp.maximum(n, 1).astype(jnp.float32)
    large = max_exact + (jnp.log(nf / max_exact) / math.log(REL_MAX_DIST / max_exact)
                         * (REL_BUCKETS - max_exact)).astype(jnp.int32)
    large = jnp.minimum(large, REL_BUCKETS - 1)
    return jnp.where(n < max_exact, n, large)


def chunked_sgu(u, v, gate, ln_g, ln_b, w_s, b_s, w_proj):
    bsz, s_len, _ = u.shape
    u = jax.nn.gelu(u)
    v = layernorm(jax.nn.gelu(v), ln_g, ln_b)
    v = v.reshape(bsz, s_len // A_CHUNK, A_CHUNK, A_GROUPS, A_WIDTH // A_GROUPS)
    tri = jnp.tril(jnp.ones((A_CHUNK, A_CHUNK), dtype=bool))
    ws = jnp.where(tri[None], w_s, 0.0)
    mixed = jnp.einsum('gts,bnsgc->bntgc', ws, v) + b_s.T[:, :, None]
    y = u * mixed.reshape(bsz, s_len, A_WIDTH) * jax.nn.silu(gate)
    return y @ w_proj


def multiscale_pool(xc, gate, w_grp, ls, w_proj):
    bsz, s_len, _ = xc.shape
    xf = xc.astype(jnp.float32)
    cs = jnp.concatenate([jnp.zeros((bsz, 1, C_WIDTH), jnp.float32),
                          jnp.cumsum(xf, axis=1)], axis=1)
    t = jnp.arange(s_len)
    outs = []
    for gi, w in enumerate(POOL_WINDOWS):
        lo = jnp.maximum(t + 1 - w, 0)
        cnt = (t + 1 - lo).astype(jnp.float32)
        sl = slice(gi * C_GROUP, (gi + 1) * C_GROUP)
        mean = (cs[:, 1:, sl] - cs[:, lo, sl]) / cnt[None, :, None]
        outs.append(mean - xf[:, :, sl])
    y = jnp.stack(outs, axis=2).astype(xc.dtype)
    y = jnp.einsum('bsgc,gcd->bsgd', y, w_grp).reshape(bsz, s_len, C_WIDTH) * ls
    return (y * jax.nn.silu(gate)) @ w_proj


def native_sparse_attention(q, kv, gsel, gate, w_cmp1, w_cmp2, pos_cmp, rel_bias, w_proj):
    bsz, s_len, _ = q.shape
    G, dh = B_KV_GROUPS, B_HEAD_DIM
    hg = B_HEADS // G
    scale = dh ** -0.5
    q = q.reshape(bsz, s_len, G, hg, dh).transpose(0, 2, 3, 1, 4)
    kv = kv.reshape(bsz, s_len, 6, G, dh).transpose(2, 0, 3, 1, 4)
    k_c, v_c, k_s, v_s, k_w, v_w = kv[0], kv[1], kv[2], kv[3], kv[4], kv[5]
    gsel = jax.nn.sigmoid(gsel.reshape(bsz, s_len, 3, G, hg)).transpose(2, 0, 3, 4, 1)

    ratio = CMP_BLOCK // CMP_STRIDE
    n_cmp = s_len // CMP_STRIDE - ratio + 1

    def compress(k, i):
        ch = k.reshape(bsz, G, s_len // CMP_STRIDE, CMP_STRIDE, dh)
        blk = jnp.concatenate([ch[:, :, j:j + n_cmp] for j in range(ratio)], axis=3) + pos_cmp[i]
        hdn = jax.nn.gelu(jnp.einsum('bgnld,lde->bgne', blk, w_cmp1[i]))
        return hdn @ w_cmp2[i]

    kc = compress(k_c, 0)
    vc = compress(v_c, 1)
    cmp_end = jnp.arange(n_cmp) * CMP_STRIDE + CMP_BLOCK - 1

    n_sel = s_len // SEL_BLOCK
    k_top = min(N_SELECT, n_sel)
    cst = np.arange(n_cmp) * CMP_STRIDE
    sst = np.arange(n_sel) * SEL_BLOCK
    overlap = jnp.asarray(((cst[:, None] < sst[None] + SEL_BLOCK) &
                           (cst[:, None] + CMP_BLOCK > sst[None])).astype(np.float32))
    ks_blk = k_s.reshape(bsz, G, n_sel, SEL_BLOCK, dh)
    vs_blk = v_s.reshape(bsz, G, n_sel, SEL_BLOCK, dh)
    sidx = jnp.arange(n_sel)
    gather_blocks = jax.vmap(jax.vmap(lambda kb, i: kb[i]))

    pad = jnp.zeros((bsz, G, WINDOW, dh), k_w.dtype)
    kw_p = jnp.concatenate([pad, k_w], axis=2)
    vw_p = jnp.concatenate([pad, v_w], axis=2)

    tab_g = rel_bias.reshape(REL_BUCKETS, G, hg).transpose(1, 0, 2)
    g_ar = jnp.arange(G)[None, :, None, None]

    def head_bias(dist):
        return jnp.moveaxis(rel_bias[t5_bucket(dist)], -1, 0).reshape(G, hg, *dist.shape)

    def block_fn(qi):
        t0 = qi * NSA_QBLOCK
        tq = t0 + jnp.arange(NSA_QBLOCK)
        qb = lax.dynamic_slice_in_dim(q, t0, NSA_QBLOCK, axis=3)

        dist_c = tq[:, None] - cmp_end[None, :]
        s = jnp.einsum('bghqd,bgnd->bghqn', qb, kc) * scale + head_bias(jnp.maximum(dist_c, 0))
        p_c = masked_softmax(s, dist_c >= 0)
        o_cmp = jnp.einsum('bghqn,bgnd->bghqd', p_c.astype(vc.dtype), vc)

        imp = jnp.einsum('bghqn,ns->bgqs', p_c, overlap)
        cur = tq // SEL_BLOCK
        forced = (sidx[None] == 0) | (sidx[None] == cur[:, None]) | (sidx[None] == cur[:, None] - 1)
        future = sidx[None] * SEL_BLOCK > tq[:, None]
        imp = jnp.where(future, -1.0, imp + jnp.where(forced, FORCE_BONUS, 0.0))
        _, idx = lax.top_k(imp, k_top)
        kg = gather_blocks(ks_blk, idx).reshape(bsz, G, NSA_QBLOCK, k_top * SEL_BLOCK, dh)
        vg = gather_blocks(vs_blk, idx).reshape(bsz, G, NSA_QBLOCK, k_top * SEL_BLOCK, dh)
        kpos = (idx[..., None] * SEL_BLOCK + jnp.arange(SEL_BLOCK)).reshape(bsz, G, NSA_QBLOCK, -1)
        dsel = tq[None, None, :, None] - kpos
        bias_sel = jnp.moveaxis(tab_g[g_ar, t5_bucket(jnp.maximum(dsel, 0))], -1, 2)
        s = jnp.einsum('bghqd,bgqkd->bghqk', qb, kg) * scale + bias_sel
        p_s = masked_softmax(s, (dsel >= 0)[:, :, None])
        o_sel = jnp.einsum('bghqk,bgqkd->bghqd', p_s.astype(vg.dtype), vg)

        kw = lax.dynamic_slice_in_dim(kw_p, t0, WINDOW + NSA_QBLOCK, axis=2)
        vw = lax.dynamic_slice_in_dim(vw_p, t0, WINDOW + NSA_QBLOCK, axis=2)
        kpos_w = t0 - WINDOW + jnp.arange(WINDOW + NSA_QBLOCK)
        dw = tq[:, None] - kpos_w[None]
        mask_w = (dw >= 0) & (dw < WINDOW) & (kpos_w[None] >= 0)
        s = jnp.einsum('bghqd,bgkd->bghqk', qb, kw) * scale + head_bias(jnp.maximum(dw, 0))
        p_w = masked_softmax(s, mask_w)
        o_win = jnp.einsum('bghqk,bgkd->bghqd', p_w.astype(vw.dtype), vw)

        g = lax.dynamic_slice_in_dim(gsel, t0, NSA_QBLOCK, axis=4)
        return g[0][..., None] * o_cmp + g[1][..., None] * o_sel + g[2][..., None] * o_win

    o = lax.map(block_fn, jnp.arange(s_len // NSA_QBLOCK))
    o = o.transpose(1, 0, 4, 2, 3, 5).reshape(bsz, s_len, B_WIDTH)
    return (o * jax.nn.silu(gate)) @ w_proj


def setup_inputs(seed: int = 0) -> dict:
    key = jax.random.key(seed)
    ks = jax.random.split(key, 20)

    def nrm(k, shape, s):
        return jax.random.normal(k, shape, jnp.float32) * s

    return {
        'x': nrm(ks[0], (BATCH, SEQ, D_MODEL), 1.0),
        'c': nrm(ks[1], (BATCH, D_MODEL), 1.0),
        'rel_bias': nrm(ks[2], (REL_BUCKETS, B_HEADS), 0.5),
        'norm_g': 1.0 + nrm(ks[3], (DEPTH, D_MODEL), 0.05),
        'w_ada': nrm(ks[4], (DEPTH, D_MODEL, 3 * D_MODEL), 0.5 * D_MODEL ** -0.5),
        'b_ada': nrm(ks[5], (DEPTH, 3 * D_MODEL), 0.01),
        'w_in': nrm(ks[6], (DEPTH, D_MODEL, N_IN_COLS), D_MODEL ** -0.5),
        'a_ln_g': 1.0 + nrm(ks[7], (DEPTH, A_WIDTH), 0.05),
        'a_ln_b': nrm(ks[8], (DEPTH, A_WIDTH), 0.01),
        'a_w_s': nrm(ks[9], (DEPTH, A_GROUPS, A_CHUNK, A_CHUNK), 0.5 * A_CHUNK ** -0.5),
        'a_b_s': 1.0 + nrm(ks[10], (DEPTH, A_GROUPS, A_CHUNK), 0.01),
        'b_w_cmp1': nrm(ks[11], (DEPTH, 2, CMP_BLOCK, B_HEAD_DIM, B_HEAD_DIM), (CMP_BLOCK * B_HEAD_DIM) ** -0.5),
        'b_w_cmp2': nrm(ks[12], (DEPTH, 2, B_HEAD_DIM, B_HEAD_DIM), B_HEAD_DIM ** -0.5),
        'b_pos_cmp': nrm(ks[13], (DEPTH, 2, CMP_BLOCK, B_HEAD_DIM), 0.1),
        'c_w_grp': nrm(ks[14], (DEPTH, len(POOL_WINDOWS), C_GROUP, C_GROUP), C_GROUP ** -0.5),
        'c_scale': 1.0 + nrm(ks[15], (DEPTH, C_WIDTH), 0.05),
        'w_branch': nrm(ks[16], (DEPTH, 3, BRANCH_WIDTH, D_MODEL), BRANCH_WIDTH ** -0.5),
        'w_out': nrm(ks[17], (DEPTH, D_MODEL, D_MODEL), D_MODEL ** -0.5),
        'final_g': 1.0 + nrm(ks[18], (D_MODEL,), 0.05),
    }


def reference(x, c, rel_bias, norm_g, w_ada, b_ada, w_in, a_ln_g, a_ln_b, a_w_s, a_b_s,
              b_w_cmp1, b_w_cmp2, b_pos_cmp, c_w_grp, c_scale, w_branch, w_out, final_g):
    for l in range(DEPTH):
        mod = jax.nn.silu(c) @ w_ada[l] + b_ada[l]
        shift, scl, gate = jnp.split(mod, 3, axis=-1)
        h = rmsnorm(x, norm_g[l]) * (1.0 + scl[:, None, :]) + shift[:, None, :]
        z = h @ w_in[l]
        (a_u, a_v, a_gate, b_q, b_kv, b_gsel, b_gate,
         c_x, c_gate, merge) = jnp.split(z, IN_SPLITS, axis=-1)
        y_a = chunked_sgu(a_u, a_v, a_gate, a_ln_g[l], a_ln_b[l], a_w_s[l], a_b_s[l], w_branch[l, 0])
        y_b = native_sparse_attention(b_q, b_kv, b_gsel, b_gate, b_w_cmp1[l], b_w_cmp2[l],
                                      b_pos_cmp[l], rel_bias, w_branch[l, 1])
        y_c = multiscale_pool(c_x, c_gate, c_w_grp[l], c_scale[l], w_branch[l, 2])
        m_a, m_b, m_c = jnp.split(jax.nn.sigmoid(merge), 3, axis=-1)
        y = m_a * y_a + m_b * y_b + m_c * y_c
        x = x + gate[:, None, :] * (y @ w_out[l])
    return rmsnorm(x, final_g)
```

```python
import functools
import math

import numpy as np
import jax
import jax.numpy as jnp
from jax import lax
from jax.experimental import pallas as pl
from jax.experimental.pallas import tpu as pltpu

D_MODEL = 2048
A_GROUPS = 8
A_CHUNK = 128
HEAD_DIM = 128
N_HEADS = 16
KV_GROUPS = 4
HEADS_PER_GROUP = N_HEADS // KV_GROUPS
CMP_BLOCK = 32
CMP_STRIDE = 16
SEL_BLOCK = 64
N_SELECT = 16
WINDOW = 512
FORCE_BONUS = 1e4
POOL_WINDOWS = (2, 4, 8, 16)
C_GROUP = D_MODEL // len(POOL_WINDOWS)
REL_BUCKETS = 32
REL_MAX_DIST = 128
EPS = 1e-6
NEG_INF = -1e30

V7X_LANES = 128
V7X_VMEM_BYTES = 64 * 1024 * 1024
VMEM_LIMIT = 48 * 1024 * 1024

ATT_TILE = 128
POOL_HALO = 16
BF16 = jnp.bfloat16
F32 = jnp.float32

ZF_U, ZF_V, ZF_AG = 0, 2048, 4096
ZF_BG = 6144
ZF_CX, ZF_CG = 8192, 10240
ZF_MERGE = 12288
ZF_KVC = 18432
ZF_GSEL = 19456
ZF_COLS = 19968
ZH_COLS = 4096


def _dot(a, b):
    return jnp.dot(a, b, preferred_element_type=F32)


def _dot_nt(a, b):
    return lax.dot_general(a, b, (((1,), (1,)), ((), ())), preferred_element_type=F32)


def _params(*sem):
    return pltpu.CompilerParams(dimension_semantics=sem, vmem_limit_bytes=VMEM_LIMIT)


def _ada_kernel(c_ref, w_ref, b_ref, o_ref):
    a = jax.nn.silu(c_ref[...]).astype(BF16)
    o_ref[...] = _dot(a, w_ref[...].astype(BF16)) + b_ref[...]


def _ada(c, w_ada, b_ada):
    depth, d, n = w_ada.shape
    bsz = c.shape[0]
    tn = 512
    return pl.pallas_call(
        _ada_kernel,
        grid=(depth, n // tn),
        in_specs=[
            pl.BlockSpec((bsz, d), lambda l, j: (0, 0)),
            pl.BlockSpec((None, d, tn), lambda l, j: (l, 0, j)),
            pl.BlockSpec((None, 1, tn), lambda l, j: (l, 0, j)),
        ],
        out_specs=pl.BlockSpec((None, bsz, tn), lambda l, j: (l, 0, j)),
        out_shape=jax.ShapeDtypeStruct((depth, bsz, n), F32),
        compiler_params=_params("parallel", "parallel"),
        name="ada_mod",
    )(c, w_ada, b_ada.reshape(depth, 1, n))


def _prenorm_kernel(x_ref, g_ref, mod_ref, o_ref):
    x = x_ref[...]
    y = x * lax.rsqrt(jnp.mean(x * x, axis=-1, keepdims=True) + EPS) * g_ref[...]
    m = mod_ref[...]
    o_ref[...] = (y * (1.0 + m[1:2]) + m[0:1]).astype(o_ref.dtype)


def _prenorm(x2, g, mod3, seq):
    m, d = x2.shape
    tm = 512
    per_seq = seq // tm
    return pl.pallas_call(
        _prenorm_kernel,
        grid=(m // tm,),
        in_specs=[
            pl.BlockSpec((tm, d), lambda i: (i, 0)),
            pl.BlockSpec((1, d), lambda i: (0, 0)),
            pl.BlockSpec((None, 3, d), lambda i: (i // per_seq, 0, 0)),
        ],
        out_specs=pl.BlockSpec((tm, d), lambda i: (i, 0)),
        out_shape=jax.ShapeDtypeStruct((m, d), BF16),
        compiler_params=_params("parallel"),
        name="prenorm",
    )(x2, g.reshape(1, d), mod3)


def _proj_kernel(a_ref, w_ref, o_ref, *, scale, n_scaled):
    acc = _dot(a_ref[...], w_ref[...])
    if n_scaled:
        acc = acc * jnp.where(pl.program_id(1) < n_scaled, scale, 1.0).astype(F32)
    o_ref[...] = acc.astype(o_ref.dtype)


def _proj(a, w, out_dtype, scale=1.0, n_scaled=0, tm=1024, tn=512):
    m, k = a.shape
    n = w.shape[1]
    tm = min(tm, m)
    return pl.pallas_call(
        functools.partial(_proj_kernel, scale=scale, n_scaled=n_scaled),
        grid=(m // tm, n // tn),
        in_specs=[
            pl.BlockSpec((tm, k), lambda i, j: (i, 0)),
            pl.BlockSpec((k, tn), lambda i, j: (0, j)),
        ],
        out_specs=pl.BlockSpec((tm, tn), lambda i, j: (i, j)),
        out_shape=jax.ShapeDtypeStruct((m, n), out_dtype),
        compiler_params=_params("parallel", "arbitrary"),
        name="in_proj",
    )(a, w)


def _sgu_kernel(u_ref, v_ref, gate_ref, lng_ref, lnb_ref, ws_ref, bst_ref, o_ref):
    tm = u_ref.shape[0]
    cg = D_MODEL // A_GROUPS
    v = jax.nn.gelu(v_ref[...])
    mu = jnp.mean(v, axis=-1, keepdims=True)
    vc = v - mu
    var = jnp.mean(vc * vc, axis=-1, keepdims=True)
    vn = (vc * lax.rsqrt(var + EPS) * lng_ref[...] + lnb_ref[...]).astype(BF16)
    row = lax.broadcasted_iota(jnp.int32, (A_CHUNK, A_CHUNK), 0)
    col = lax.broadcasted_iota(jnp.int32, (A_CHUNK, A_CHUNK), 1)
    tril = row >= col
    bst = bst_ref[...]
    for g in range(A_GROUPS):
        ws = jnp.where(tril, ws_ref[g], 0.0).astype(BF16)
        cs = slice(g * cg, (g + 1) * cg)
        for c in range(tm // A_CHUNK):
            rs = slice(c * A_CHUNK, (c + 1) * A_CHUNK)
            mixed = _dot(ws, vn[rs, cs]) + bst[:, g:g + 1]
            y = jax.nn.gelu(u_ref[rs, cs]) * mixed * jax.nn.silu(gate_ref[rs, cs])
            o_ref[rs, cs] = y.astype(o_ref.dtype)


def _sgu(zf, ln_g, ln_b, w_s, b_s):
    m = zf.shape[0]
    tm = 256
    d = D_MODEL
    return pl.pallas_call(
        _sgu_kernel,
        grid=(m // tm,),
        in_specs=[
            pl.BlockSpec((tm, d), lambda i: (i, ZF_U // d)),
            pl.BlockSpec((tm, d), lambda i: (i, ZF_V // d)),
            pl.BlockSpec((tm, d), lambda i: (i, ZF_AG // d)),
            pl.BlockSpec((1, d), lambda i: (0, 0)),
            pl.BlockSpec((1, d), lambda i: (0, 0)),
            pl.BlockSpec((A_GROUPS, A_CHUNK, A_CHUNK), lambda i: (0, 0, 0)),
            pl.BlockSpec((A_CHUNK, A_GROUPS), lambda i: (0, 0)),
        ],
        out_specs=pl.BlockSpec((tm, d), lambda i: (i, 0)),
        out_shape=jax.ShapeDtypeStruct((m, d), BF16),
        compiler_params=_params("parallel"),
        name="sgu",
    )(zf, zf, zf, ln_g.reshape(1, d), ln_b.reshape(1, d), w_s, b_s.T)


def _pool_kernel(x_ref, halo_ref, gate_ref, wg_ref, ls_ref, o_ref, ext_ref, *, tiles_per_seq):
    tm = x_ref.shape[0]
    i = pl.program_id(0)
    pos0 = (i % tiles_per_seq) * tm
    halo = halo_ref[...]
    ext_ref[0:POOL_HALO, :] = jnp.where(pos0 == 0, jnp.zeros_like(halo), halo)
    ext_ref[POOL_HALO:POOL_HALO + tm, :] = x_ref[...]
    tpos = pos0 + lax.broadcasted_iota(jnp.int32, (tm, 1), 0)
    for gi, w in enumerate(POOL_WINDOWS):
        cs = slice(gi * C_GROUP, (gi + 1) * C_GROUP)
        x = ext_ref[POOL_HALO:POOL_HALO + tm, cs]
        s = x
        for back in range(1, w):
            s = s + ext_ref[POOL_HALO - back:POOL_HALO - back + tm, cs]
        cnt = jnp.minimum(tpos + 1, w).astype(F32)
        y = (s / cnt - x).astype(BF16)
        yg = _dot(y, wg_ref[gi].astype(BF16)) * ls_ref[:, cs] * jax.nn.silu(gate_ref[:, cs])
        o_ref[:, cs] = yg.astype(o_ref.dtype)


def _pool(zf, w_grp, ls, seq):
    m = zf.shape[0]
    tm = 256
    d = D_MODEL
    halo_blocks = tm // POOL_HALO
    return pl.pallas_call(
        functools.partial(_pool_kernel, tiles_per_seq=seq // tm),
        grid=(m // tm,),
        in_specs=[
            pl.BlockSpec((tm, d), lambda i: (i, ZF_CX // d)),
            pl.BlockSpec((POOL_HALO, d), lambda i: (jnp.maximum(i * halo_blocks - 1, 0), ZF_CX // d)),
            pl.BlockSpec((tm, d), lambda i: (i, ZF_CG // d)),
            pl.BlockSpec((len(POOL_WINDOWS), C_GROUP, C_GROUP), lambda i: (0, 0, 0)),
            pl.BlockSpec((1, d), lambda i: (0, 0)),
        ],
        out_specs=pl.BlockSpec((tm, d), lambda i: (i, 0)),
        out_shape=jax.ShapeDtypeStruct((m, d), BF16),
        scratch_shapes=[pltpu.VMEM((POOL_HALO + tm, d), F32)],
        compiler_params=_params("parallel"),
        name="pool",
    )(zf, zf, zf, w_grp, ls.reshape(1, d))


def _compress_kernel(x_ref, w1_ref, w2_ref, pos_ref, o_ref):
    n_chunks = x_ref.shape[0] // CMP_STRIDE
    pos = pos_ref[...]
    first = jnp.zeros((n_chunks, HEAD_DIM), F32)
    second = jnp.zeros((n_chunks, HEAD_DIM), F32)
    for l in range(CMP_STRIDE):
        xl = x_ref[pl.ds(l, n_chunks, stride=CMP_STRIDE), :]
        first += _dot((xl + pos[l:l + 1]).astype(BF16), w1_ref[l].astype(BF16))
        second += _dot((xl + pos[CMP_STRIDE + l:CMP_STRIDE + l + 1]).astype(BF16),
                       w1_ref[CMP_STRIDE + l].astype(BF16))
    hdn = jax.nn.gelu(first + pltpu.roll(second, n_chunks - 1, axis=0))
    o_ref[...] = _dot(hdn.astype(BF16), w2_ref[...].astype(BF16)).astype(o_ref.dtype)


def _compress(zf, w1, w2, pos, bsz, seq):
    n_chunks = seq // CMP_STRIDE
    kvc_block0 = ZF_KVC // HEAD_DIM
    return pl.pallas_call(
        _compress_kernel,
        grid=(bsz, 2, KV_GROUPS),
        in_specs=[
            pl.BlockSpec((seq, HEAD_DIM), lambda b, t, g: (b, kvc_block0 + t * KV_GROUPS + g)),
            pl.BlockSpec((None, CMP_BLOCK, HEAD_DIM, HEAD_DIM), lambda b, t, g: (t, 0, 0, 0)),
            pl.BlockSpec((None, HEAD_DIM, HEAD_DIM), lambda b, t, g: (t, 0, 0)),
            pl.BlockSpec((None, CMP_BLOCK, HEAD_DIM), lambda b, t, g: (t, 0, 0)),
        ],
        out_specs=pl.BlockSpec((None, None, None, n_chunks, HEAD_DIM), lambda b, t, g: (b, t, g, 0, 0)),
        out_shape=jax.ShapeDtypeStruct((bsz, 2, KV_GROUPS, n_chunks, HEAD_DIM), BF16),
        compiler_params=_params("parallel", "parallel", "parallel"),
        name="compress",
    )(zf, w1, w2, pos)


def _bucket_of_distance(n):
    n = np.asarray(n, np.int64)
    max_exact = REL_BUCKETS // 2
    nf = np.maximum(n, 1).astype(np.float64)
    large = max_exact + (np.log(nf / max_exact) / math.log(REL_MAX_DIST / max_exact)
                         * (REL_BUCKETS - max_exact)).astype(np.int64)
    large = np.minimum(large, REL_BUCKETS - 1)
    return np.where(n < max_exact, n, large).astype(np.int32)


def _bias_kernel(tab_ref, idx_near_ref, idx_cmp_ref, near_ref, cmp_ref):
    h = pl.program_id(0) * HEADS_PER_GROUP + pl.program_id(1)
    for idx_ref, out_ref in ((idx_near_ref, near_ref), (idx_cmp_ref, cmp_ref)):
        idx = idx_ref[...]
        acc = jnp.zeros(idx.shape, F32)
        for b in range(REL_BUCKETS):
            acc = jnp.where(idx == b, tab_ref[b, h], acc)
        out_ref[...] = acc


def _bias_tables(rel_bias, seq):
    t = ATT_TILE
    n_q = seq // t
    r = np.arange(t)[:, None]
    c = np.arange(t)[None, :]
    idx_near = np.stack([_bucket_of_distance(np.maximum(r - c, 0)),
                         _bucket_of_distance(t + r - c),
                         np.broadcast_to(_bucket_of_distance(2 * t), (t, t))]).astype(np.int32)
    cmp_end = c * CMP_STRIDE + CMP_BLOCK - 1
    idx_cmp = np.stack([_bucket_of_distance(np.maximum(i * t + r - cmp_end, 0))
                        for i in range(n_q)]).astype(np.int32)
    return pl.pallas_call(
        _bias_kernel,
        grid=(KV_GROUPS, HEADS_PER_GROUP),
        in_specs=[
            pl.BlockSpec(memory_space=pltpu.SMEM),
            pl.BlockSpec((3, t, t), lambda g, j: (0, 0, 0)),
            pl.BlockSpec((n_q, t, t), lambda g, j: (0, 0, 0)),
        ],
        out_specs=[
            pl.BlockSpec((None, 3, t, t), lambda g, j: (g, 0, j, 0)),
            pl.BlockSpec((None, n_q, t, t), lambda g, j: (g, 0, j, 0)),
        ],
        out_shape=[
            jax.ShapeDtypeStruct((KV_GROUPS, 3, HEADS_PER_GROUP * t, t), F32),
            jax.ShapeDtypeStruct((KV_GROUPS, n_q, HEADS_PER_GROUP * t, t), F32),
        ],
        compiler_params=_params("parallel", "parallel"),
        name="bias_tables",
    )(rel_bias, jnp.asarray(idx_near), jnp.asarray(idx_cmp))


def _attn_kernel(q_ref, ks_ref, vs_ref, kw_ref, vw_ref, kc_ref, vc_ref, gsel_ref, bgate_ref,
                 bnear_ref, bcmp_ref, ovt_ref, expand_ref, eye_ref, o_ref, mask_ref):
    t = ATT_TILE
    hg = HEADS_PER_GROUP
    rows = hg * t
    i = pl.program_id(1)
    n_tiles = mask_ref.shape[0]

    qb = q_ref[...]
    qs = jnp.concatenate([qb[:, j * HEAD_DIM:(j + 1) * HEAD_DIM] for j in range(hg)], axis=0)

    row_in_tile = lax.broadcasted_iota(jnp.int32, (rows, t), 0) & (t - 1)
    col = lax.broadcasted_iota(jnp.int32, (rows, t), 1)
    rc = row_in_tile - col

    s = _dot_nt(qs, kc_ref[...]) + bcmp_ref[...]
    valid = (i * t + row_in_tile) >= (col * CMP_STRIDE + (CMP_BLOCK - 1))
    s = jnp.where(valid, s, NEG_INF)
    e = jnp.where(valid, jnp.exp(s - jnp.max(s, axis=-1, keepdims=True)), 0.0)
    l = jnp.sum(e, axis=-1, keepdims=True)
    p_c = e * (1.0 / jnp.where(l > 0.0, l, 1.0))
    o_cmp = _dot(p_c.astype(BF16), vc_ref[...])

    psum = p_c[0:t]
    for j in range(1, hg):
        psum = psum + p_c[j * t:(j + 1) * t]
    p_hi = psum.astype(BF16)
    p_lo = (psum - p_hi.astype(F32)).astype(BF16)
    ovt = ovt_ref[...]
    n_sel = expand_ref.shape[1] // SEL_BLOCK
    imp = (_dot_nt(ovt, p_hi) + _dot_nt(ovt, p_lo))[0:n_sel]
    blk = lax.broadcasted_iota(jnp.int32, (n_sel, t), 0)
    tq = i * t + lax.broadcasted_iota(jnp.int32, (n_sel, t), 1)
    cur = lax.shift_right_logical(tq, int(math.log2(SEL_BLOCK)))
    forced = (blk == 0) | (blk == cur) | (blk == cur - 1)
    future = blk * SEL_BLOCK > tq
    imp = jnp.where(future, -1.0, imp + jnp.where(forced, FORCE_BONUS, 0.0))
    rank = jnp.zeros((n_sel, t), F32)
    for sp in range(n_sel):
        other = imp[sp:sp + 1, :]
        beats = (other > imp) | ((other == imp) & (blk > sp))
        rank = rank + jnp.where(beats, 1.0, 0.0)
    sel = jnp.where(rank < float(min(N_SELECT, n_sel)), 1.0, 0.0)
    sel_pad = jnp.concatenate([sel, jnp.zeros((t - n_sel, t), F32)], axis=0).astype(BF16)
    sel_qb = _dot_nt(eye_ref[...], sel_pad).astype(BF16)
    for j in range(n_tiles):
        mask_ref[j] = _dot(sel_qb, expand_ref[:, j * t:(j + 1) * t])

    def attend(k_ref, v_ref, j_lo, j_hi, selected):
        def body(j, carry):
            m_run, l_run, acc = carry
            start = pl.multiple_of(j * t, t)
            k = k_ref[pl.ds(start, t), :]
            v = v_ref[pl.ds(start, t), :]
            delta = i - j
            s = _dot_nt(qs, k) + bnear_ref[jnp.minimum(delta, 2)]
            d = rc + delta * t
            if selected:
                msk = mask_ref[j]
                valid = (d >= 0) & (jnp.concatenate([msk] * hg, axis=0) > 0.5)
            else:
                valid = (d >= 0) & (d < WINDOW)
            s = jnp.where(valid, s, NEG_INF)
            m_new = jnp.maximum(m_run, jnp.max(s, axis=-1, keepdims=True))
            alpha = jnp.exp(m_run - m_new)
            p = jnp.where(valid, jnp.exp(s - m_new), 0.0)
            l_new = alpha * l_run + jnp.sum(p, axis=-1, keepdims=True)
            acc = alpha * acc + _dot(p.astype(BF16), v)
            return m_new, l_new, acc

        init = (jnp.full((rows, 1), NEG_INF, F32), jnp.zeros((rows, 1), F32),
                jnp.zeros((rows, HEAD_DIM), F32))
        _, l_fin, acc = lax.fori_loop(j_lo, j_hi, body, init)
        return acc * (1.0 / jnp.where(l_fin > 0.0, l_fin, 1.0))

    o_sel = attend(ks_ref, vs_ref, 0, i + 1, True)
    o_win = attend(kw_ref, vw_ref, jnp.maximum(i - WINDOW // t, 0), i + 1, False)

    gates = jax.nn.sigmoid(gsel_ref[...])
    for j in range(hg):
        rs = slice(j * t, (j + 1) * t)
        o = (gates[:, j:j + 1] * o_cmp[rs]
             + gates[:, hg + j:hg + j + 1] * o_sel[rs]
             + gates[:, 2 * hg + j:2 * hg + j + 1] * o_win[rs])
        cs = slice(j * HEAD_DIM, (j + 1) * HEAD_DIM)
        o_ref[:, cs] = (o * jax.nn.silu(bgate_ref[:, cs])).astype(o_ref.dtype)


def _attention_constants(seq):
    t = ATT_TILE
    n_cmp_pad = seq // CMP_STRIDE
    assert n_cmp_pad == t, "one tile of compressed keys per sequence"
    n_cmp = n_cmp_pad - CMP_BLOCK // CMP_STRIDE + 1
    n_sel = seq // SEL_BLOCK
    cst = np.arange(n_cmp_pad) * CMP_STRIDE
    sst = np.arange(n_sel) * SEL_BLOCK
    overlap = ((cst[:, None] < sst[None] + SEL_BLOCK) & (cst[:, None] + CMP_BLOCK > sst[None]))
    overlap[n_cmp:] = False
    ovt = np.zeros((t, n_cmp_pad), np.float32)
    ovt[:n_sel] = overlap.T
    expand = np.zeros((t, seq), np.float32)
    expand[np.arange(seq) // SEL_BLOCK, np.arange(seq)] = 1.0
    eye = np.eye(t, dtype=np.float32)
    return (jnp.asarray(ovt, BF16), jnp.asarray(expand, BF16), jnp.asarray(eye, BF16))


def _attention(zh, zf, kvc, bias_near, bias_cmp, consts, bsz, seq):
    t = ATT_TILE
    n_q = seq // t
    gw = HEADS_PER_GROUP * HEAD_DIM
    kv0 = D_MODEL // HEAD_DIM
    ovt, expand, eye = consts

    def kv_spec(kind):
        return pl.BlockSpec((seq, HEAD_DIM),
                            lambda bg, i: (bg // KV_GROUPS, kv0 + kind * KV_GROUPS + bg % KV_GROUPS))

    def cmp_spec(kind):
        return pl.BlockSpec((None, None, None, t, HEAD_DIM),
                            lambda bg, i: (bg // KV_GROUPS, kind, bg % KV_GROUPS, 0, 0))

    def tok(bg, i):
        return (bg // KV_GROUPS) * n_q + i

    return pl.pallas_call(
        _attn_kernel,
        grid=(bsz * KV_GROUPS, n_q),
        in_specs=[
            pl.BlockSpec((t, gw), lambda bg, i: (tok(bg, i), bg % KV_GROUPS)),
            kv_spec(0), kv_spec(1), kv_spec(2), kv_spec(3),
            cmp_spec(0), cmp_spec(1),
            pl.BlockSpec((t, V7X_LANES), lambda bg, i: (tok(bg, i), ZF_GSEL // V7X_LANES + bg % KV_GROUPS)),
            pl.BlockSpec((t, gw), lambda bg, i: (tok(bg, i), ZF_BG // gw + bg % KV_GROUPS)),
            pl.BlockSpec((None, 3, HEADS_PER_GROUP * t, t), lambda bg, i: (bg % KV_GROUPS, 0, 0, 0)),
            pl.BlockSpec((None, None, HEADS_PER_GROUP * t, t), lambda bg, i: (bg % KV_GROUPS, i, 0, 0)),
            pl.BlockSpec((t, t), lambda bg, i: (0, 0)),
            pl.BlockSpec((t, seq), lambda bg, i: (0, 0)),
            pl.BlockSpec((t, t), lambda bg, i: (0, 0)),
        ],
        out_specs=pl.BlockSpec((t, gw), lambda bg, i: (tok(bg, i), bg % KV_GROUPS)),
        out_shape=jax.ShapeDtypeStruct((bsz * seq, D_MODEL), BF16),
        scratch_shapes=[pltpu.VMEM((n_q, t, t), F32)],
        compiler_params=_params("parallel", "arbitrary"),
        name="nsa_attention",
    )(zh, zh, zh, zh, zh, kvc, kvc, zf, zf, bias_near, bias_cmp, ovt, expand, eye)


def _merge_kernel(ya_ref, yb_ref, yc_ref, ma_ref, mb_ref, mc_ref, w_ref, o_ref):
    y = jax.nn.sigmoid(ma_ref[...]) * _dot(ya_ref[...], w_ref[0])
    y += jax.nn.sigmoid(mb_ref[...]) * _dot(yb_ref[...], w_ref[1])
    y += jax.nn.sigmoid(mc_ref[...]) * _dot(yc_ref[...], w_ref[2])
    o_ref[...] = y.astype(o_ref.dtype)


def _merge(ya, yb, yc, zf, w_branch):
    m, d = ya.shape
    tm, tn = 512, 512
    m0 = ZF_MERGE // tn
    per = d // tn
    y_spec = pl.BlockSpec((tm, d), lambda i, j: (i, 0))
    return pl.pallas_call(
        _merge_kernel,
        grid=(m // tm, d // tn),
        in_specs=[
            y_spec, y_spec, y_spec,
            pl.BlockSpec((tm, tn), lambda i, j: (i, m0 + j)),
            pl.BlockSpec((tm, tn), lambda i, j: (i, m0 + per + j)),
            pl.BlockSpec((tm, tn), lambda i, j: (i, m0 + 2 * per + j)),
            pl.BlockSpec((3, d, tn), lambda i, j: (0, 0, j)),
        ],
        out_specs=pl.BlockSpec((tm, tn), lambda i, j: (i, j)),
        out_shape=jax.ShapeDtypeStruct((m, d), BF16),
        compiler_params=_params("parallel", "arbitrary"),
        name="merge",
    )(ya, yb, yc, zf, zf, zf, w_branch)


def _out_kernel(y_ref, w_ref, x_ref, mod_ref, o_ref):
    o_ref[...] = x_ref[...] + mod_ref[2:3, :] * _dot(y_ref[...], w_ref[...])


def _out_proj(y, w_out, x2, mod3, seq):
    m, d = y.shape
    tm, tn = 512, 512
    per_seq = seq // tm
    return pl.pallas_call(
        _out_kernel,
        grid=(m // tm, d // tn),
        in_specs=[
            pl.BlockSpec((tm, d), lambda i, j: (i, 0)),
            pl.BlockSpec((d, tn), lambda i, j: (0, j)),
            pl.BlockSpec((tm, tn), lambda i, j: (i, j)),
            pl.BlockSpec((None, 3, tn), lambda i, j: (i // per_seq, 0, j)),
        ],
        out_specs=pl.BlockSpec((tm, tn), lambda i, j: (i, j)),
        out_shape=jax.ShapeDtypeStruct((m, d), F32),
        compiler_params=_params("parallel", "arbitrary"),
        name="out_proj",
    )(y, w_out, x2, mod3)


def _final_norm_kernel(x_ref, g_ref, o_ref):
    x = x_ref[...]
    o_ref[...] = x * lax.rsqrt(jnp.mean(x * x, axis=-1, keepdims=True) + EPS) * g_ref[...]


def _final_norm(x2, g):
    m, d = x2.shape
    tm = 512
    return pl.pallas_call(
        _final_norm_kernel,
        grid=(m // tm,),
        in_specs=[pl.BlockSpec((tm, d), lambda i: (i, 0)), pl.BlockSpec((1, d), lambda i: (0, 0))],
        out_specs=pl.BlockSpec((tm, d), lambda i: (i, 0)),
        out_shape=jax.ShapeDtypeStruct((m, d), F32),
        compiler_params=_params("parallel"),
        name="final_norm",
    )(x2, g.reshape(1, d))


def _split_w_in(w):
    d = D_MODEL
    kvw = KV_GROUPS * HEAD_DIM
    sizes = (d, d, d, d, 6 * kvw, 3 * N_HEADS, d, d, d, 3 * d)
    offs = np.concatenate([[0], np.cumsum(sizes)])
    a_u, a_v, a_g, q, kv, gsel, b_g, c_x, c_g, merge = [w[:, offs[k]:offs[k + 1]] for k in range(10)]
    gs = gsel.reshape(d, 3, KV_GROUPS, HEADS_PER_GROUP).transpose(0, 2, 1, 3)
    gs = gs.reshape(d, KV_GROUPS, 3 * HEADS_PER_GROUP)
    gs = jnp.pad(gs, ((0, 0), (0, 0), (0, V7X_LANES - 3 * HEADS_PER_GROUP))).reshape(d, KV_GROUPS * V7X_LANES)
    wf = jnp.concatenate([a_u, a_v, a_g, b_g, c_x, c_g, merge, kv[:, :2 * kvw], gs], axis=1)
    wh = jnp.concatenate([q, kv[:, 2 * kvw:]], axis=1)
    assert wf.shape[1] == ZF_COLS and wh.shape[1] == ZH_COLS
    return wf.astype(BF16), wh.astype(BF16)


def kernel(x, c, rel_bias, norm_g, w_ada, b_ada, w_in, a_ln_g, a_ln_b, a_w_s, a_b_s, b_w_cmp1,
           b_w_cmp2, b_pos_cmp, c_w_grp, c_scale, w_branch, w_out, final_g):
    bsz, seq, d = x.shape
    depth = w_in.shape[0]
    assert d == D_MODEL and seq % 512 == 0
    x2 = x.reshape(bsz * seq, d)
    mod = _ada(c, w_ada, b_ada)
    bias_near, bias_cmp = _bias_tables(rel_bias, seq)
    consts = _attention_constants(seq)
    q_blocks = D_MODEL // 512
    for l in range(depth):
        mod3 = mod[l].reshape(bsz, 3, d)
        wf, wh = _split_w_in(w_in[l])
        h = _prenorm(x2, norm_g[l], mod3, seq)
        zf = _proj(h, wf, F32)
        zh = _proj(h, wh, BF16, scale=HEAD_DIM ** -0.5, n_scaled=q_blocks)
        ya = _sgu(zf, a_ln_g[l], a_ln_b[l], a_w_s[l], a_b_s[l])
        kvc = _compress(zf, b_w_cmp1[l], b_w_cmp2[l], b_pos_cmp[l], bsz, seq)
        yb = _attention(zh, zf, kvc, bias_near, bias_cmp, consts, bsz, seq)
        yc = _pool(zf, c_w_grp[l], c_scale[l], seq)
        y = _merge(ya, yb, yc, zf, w_branch[l].astype(BF16))
        x2 = _out_proj(y, w_out[l].astype(BF16), x2, mod3, seq)
    return _final_norm(x2, final_g).reshape(bsz, seq, d)
```

```python
import functools
import math

import numpy as np
import jax
import jax.numpy as jnp
from jax import lax
from jax.experimental import pallas as pl
from jax.experimental.pallas import tpu as pltpu

D_MODEL = 2048
A_GROUPS = 8
A_CHUNK = 128
HEAD_DIM = 128
N_HEADS = 16
KV_GROUPS = 4
HEADS_PER_GROUP = N_HEADS // KV_GROUPS
CMP_BLOCK = 32
CMP_STRIDE = 16
SEL_BLOCK = 64
N_SELECT = 16
WINDOW = 512
FORCE_BONUS = 1e4
POOL_WINDOWS = (2, 4, 8, 16)
C_GROUP = D_MODEL // len(POOL_WINDOWS)
REL_BUCKETS = 32
REL_MAX_DIST = 128
EPS = 1e-6
NEG_INF = -1e30

V7X_LANES = 128
V7X_VMEM_BYTES = 64 * 1024 * 1024
VMEM_LIMIT = 48 * 1024 * 1024

ATT_TILE = 128
KEY_TILE = 256
LOG2_E = math.log2(math.e)
POOL_HALO = 16
BF16 = jnp.bfloat16
F32 = jnp.float32

ZF_U, ZF_V, ZF_AG = 0, 2048, 4096
ZF_BG = 6144
ZF_CX, ZF_CG = 8192, 10240
ZF_MERGE = 12288
ZF_KVC = 18432
ZF_GSEL = 19456
ZF_COLS = 19968
ZH_COLS = 4096


def _dot(a, b):
    return jnp.dot(a, b, preferred_element_type=F32)


def _dot_nt(a, b):
    return lax.dot_general(a, b, (((1,), (1,)), ((), ())), preferred_element_type=F32)


def _params(*sem):
    return pltpu.CompilerParams(dimension_semantics=sem, vmem_limit_bytes=VMEM_LIMIT)


def _ada_kernel(c_ref, w_ref, b_ref, o_ref):
    a = jax.nn.silu(c_ref[...]).astype(BF16)
    o_ref[...] = _dot(a, w_ref[...].astype(BF16)) + b_ref[...]


def _ada(c, w_ada, b_ada):
    depth, d, n = w_ada.shape
    bsz = c.shape[0]
    tn = 512
    return pl.pallas_call(
        _ada_kernel,
        grid=(depth, n // tn),
        in_specs=[
            pl.BlockSpec((bsz, d), lambda l, j: (0, 0)),
            pl.BlockSpec((None, d, tn), lambda l, j: (l, 0, j)),
            pl.BlockSpec((None, 1, tn), lambda l, j: (l, 0, j)),
        ],
        out_specs=pl.BlockSpec((None, bsz, tn), lambda l, j: (l, 0, j)),
        out_shape=jax.ShapeDtypeStruct((depth, bsz, n), F32),
        compiler_params=_params("parallel", "parallel"),
        name="ada_mod",
    )(c, w_ada, b_ada.reshape(depth, 1, n))


def _prenorm_kernel(x_ref, g_ref, mod_ref, o_ref):
    x = x_ref[...]
    y = x * lax.rsqrt(jnp.mean(x * x, axis=-1, keepdims=True) + EPS) * g_ref[...]
    m = mod_ref[...]
    o_ref[...] = (y * (1.0 + m[1:2]) + m[0:1]).astype(o_ref.dtype)


def _prenorm(x2, g, mod3, seq):
    m, d = x2.shape
    tm = 512
    per_seq = seq // tm
    return pl.pallas_call(
        _prenorm_kernel,
        grid=(m // tm,),
        in_specs=[
            pl.BlockSpec((tm, d), lambda i: (i, 0)),
            pl.BlockSpec((1, d), lambda i: (0, 0)),
            pl.BlockSpec((None, 3, d), lambda i: (i // per_seq, 0, 0)),
        ],
        out_specs=pl.BlockSpec((tm, d), lambda i: (i, 0)),
        out_shape=jax.ShapeDtypeStruct((m, d), BF16),
        compiler_params=_params("parallel"),
        name="prenorm",
    )(x2, g.reshape(1, d), mod3)


def _proj_kernel(a_ref, w_ref, o_ref, *, scale, n_scaled):
    acc = _dot(a_ref[...], w_ref[...])
    if n_scaled:
        acc = acc * jnp.where(pl.program_id(1) < n_scaled, scale, 1.0).astype(F32)
    o_ref[...] = acc.astype(o_ref.dtype)


def _proj(a, w, out_dtype, scale=1.0, n_scaled=0, tm=1024, tn=512):
    m, k = a.shape
    n = w.shape[1]
    tm = min(tm, m)
    return pl.pallas_call(
        functools.partial(_proj_kernel, scale=scale, n_scaled=n_scaled),
        grid=(m // tm, n // tn),
        in_specs=[
            pl.BlockSpec((tm, k), lambda i, j: (i, 0)),
            pl.BlockSpec((k, tn), lambda i, j: (0, j)),
        ],
        out_specs=pl.BlockSpec((tm, tn), lambda i, j: (i, j)),
        out_shape=jax.ShapeDtypeStruct((m, n), out_dtype),
        compiler_params=_params("parallel", "arbitrary"),
        name="in_proj",
    )(a, w)


def _sgu_kernel(u_ref, v_ref, gate_ref, lng_ref, lnb_ref, ws_ref, bst_ref, o_ref):
    tm = u_ref.shape[0]
    cg = D_MODEL // A_GROUPS
    v = jax.nn.gelu(v_ref[...])
    mu = jnp.mean(v, axis=-1, keepdims=True)
    vc = v - mu
    var = jnp.mean(vc * vc, axis=-1, keepdims=True)
    vn = (vc * lax.rsqrt(var + EPS) * lng_ref[...] + lnb_ref[...]).astype(BF16)
    row = lax.broadcasted_iota(jnp.int32, (A_CHUNK, A_CHUNK), 0)
    col = lax.broadcasted_iota(jnp.int32, (A_CHUNK, A_CHUNK), 1)
    tril = row >= col
    bst = bst_ref[...]
    for g in range(A_GROUPS):
        ws = jnp.where(tril, ws_ref[g], 0.0).astype(BF16)
        cs = slice(g * cg, (g + 1) * cg)
        for c in range(tm // A_CHUNK):
            rs = slice(c * A_CHUNK, (c + 1) * A_CHUNK)
            mixed = _dot(ws, vn[rs, cs]) + bst[:, g:g + 1]
            y = jax.nn.gelu(u_ref[rs, cs]) * mixed * jax.nn.silu(gate_ref[rs, cs])
            o_ref[rs, cs] = y.astype(o_ref.dtype)


def _sgu(zf, ln_g, ln_b, w_s, b_s):
    m = zf.shape[0]
    tm = 256
    d = D_MODEL
    return pl.pallas_call(
        _sgu_kernel,
        grid=(m // tm,),
        in_specs=[
            pl.BlockSpec((tm, d), lambda i: (i, ZF_U // d)),
            pl.BlockSpec((tm, d), lambda i: (i, ZF_V // d)),
            pl.BlockSpec((tm, d), lambda i: (i, ZF_AG // d)),
            pl.BlockSpec((1, d), lambda i: (0, 0)),
            pl.BlockSpec((1, d), lambda i: (0, 0)),
            pl.BlockSpec((A_GROUPS, A_CHUNK, A_CHUNK), lambda i: (0, 0, 0)),
            pl.BlockSpec((A_CHUNK, A_GROUPS), lambda i: (0, 0)),
        ],
        out_specs=pl.BlockSpec((tm, d), lambda i: (i, 0)),
        out_shape=jax.ShapeDtypeStruct((m, d), BF16),
        compiler_params=_params("parallel"),
        name="sgu",
    )(zf, zf, zf, ln_g.reshape(1, d), ln_b.reshape(1, d), w_s, b_s.T)


def _pool_kernel(x_ref, halo_ref, gate_ref, wg_ref, ls_ref, o_ref, ext_ref, *, tiles_per_seq):
    tm = x_ref.shape[0]
    i = pl.program_id(0)
    pos0 = (i % tiles_per_seq) * tm
    halo = halo_ref[...]
    ext_ref[0:POOL_HALO, :] = jnp.where(pos0 == 0, jnp.zeros_like(halo), halo)
    ext_ref[POOL_HALO:POOL_HALO + tm, :] = x_ref[...]
    tpos = pos0 + lax.broadcasted_iota(jnp.int32, (tm, 1), 0)
    for gi, w in enumerate(POOL_WINDOWS):
        cs = slice(gi * C_GROUP, (gi + 1) * C_GROUP)
        x = ext_ref[POOL_HALO:POOL_HALO + tm, cs]
        s = x
        for back in range(1, w):
            s = s + ext_ref[POOL_HALO - back:POOL_HALO - back + tm, cs]
        cnt = jnp.minimum(tpos + 1, w).astype(F32)
        y = (s / cnt - x).astype(BF16)
        yg = _dot(y, wg_ref[gi].astype(BF16)) * ls_ref[:, cs] * jax.nn.silu(gate_ref[:, cs])
        o_ref[:, cs] = yg.astype(o_ref.dtype)


def _pool(zf, w_grp, ls, seq):
    m = zf.shape[0]
    tm = 256
    d = D_MODEL
    halo_blocks = tm // POOL_HALO
    return pl.pallas_call(
        functools.partial(_pool_kernel, tiles_per_seq=seq // tm),
        grid=(m // tm,),
        in_specs=[
            pl.BlockSpec((tm, d), lambda i: (i, ZF_CX // d)),
            pl.BlockSpec((POOL_HALO, d), lambda i: (jnp.maximum(i * halo_blocks - 1, 0), ZF_CX // d)),
            pl.BlockSpec((tm, d), lambda i: (i, ZF_CG // d)),
            pl.BlockSpec((len(POOL_WINDOWS), C_GROUP, C_GROUP), lambda i: (0, 0, 0)),
            pl.BlockSpec((1, d), lambda i: (0, 0)),
        ],
        out_specs=pl.BlockSpec((tm, d), lambda i: (i, 0)),
        out_shape=jax.ShapeDtypeStruct((m, d), BF16),
        scratch_shapes=[pltpu.VMEM((POOL_HALO + tm, d), F32)],
        compiler_params=_params("parallel"),
        name="pool",
    )(zf, zf, zf, w_grp, ls.reshape(1, d))


def _compress_kernel(x_ref, w1_ref, w2_ref, pos_ref, o_ref):
    n_chunks = x_ref.shape[0] // CMP_STRIDE
    pos = pos_ref[...]
    first = jnp.zeros((n_chunks, HEAD_DIM), F32)
    second = jnp.zeros((n_chunks, HEAD_DIM), F32)
    for l in range(CMP_STRIDE):
        xl = x_ref[pl.ds(l, n_chunks, stride=CMP_STRIDE), :]
        first += _dot((xl + pos[l:l + 1]).astype(BF16), w1_ref[l].astype(BF16))
        second += _dot((xl + pos[CMP_STRIDE + l:CMP_STRIDE + l + 1]).astype(BF16),
                       w1_ref[CMP_STRIDE + l].astype(BF16))
    hdn = jax.nn.gelu(first + pltpu.roll(second, n_chunks - 1, axis=0))
    o_ref[...] = _dot(hdn.astype(BF16), w2_ref[...].astype(BF16)).astype(o_ref.dtype)


def _compress(zf, w1, w2, pos, bsz, seq):
    n_chunks = seq // CMP_STRIDE
    kvc_block0 = ZF_KVC // HEAD_DIM
    return pl.pallas_call(
        _compress_kernel,
        grid=(bsz, 2, KV_GROUPS),
        in_specs=[
            pl.BlockSpec((seq, HEAD_DIM), lambda b, t, g: (b, kvc_block0 + t * KV_GROUPS + g)),
            pl.BlockSpec((None, CMP_BLOCK, HEAD_DIM, HEAD_DIM), lambda b, t, g: (t, 0, 0, 0)),
            pl.BlockSpec((None, HEAD_DIM, HEAD_DIM), lambda b, t, g: (t, 0, 0)),
            pl.BlockSpec((None, CMP_BLOCK, HEAD_DIM), lambda b, t, g: (t, 0, 0)),
        ],
        out_specs=pl.BlockSpec((None, None, None, n_chunks, HEAD_DIM), lambda b, t, g: (b, t, g, 0, 0)),
        out_shape=jax.ShapeDtypeStruct((bsz, 2, KV_GROUPS, n_chunks, HEAD_DIM), BF16),
        compiler_params=_params("parallel", "parallel", "parallel"),
        name="compress",
    )(zf, w1, w2, pos)


def _bucket_of_distance(n):
    n = np.asarray(n, np.int64)
    max_exact = REL_BUCKETS // 2
    nf = np.maximum(n, 1).astype(np.float64)
    large = max_exact + (np.log(nf / max_exact) / math.log(REL_MAX_DIST / max_exact)
                         * (REL_BUCKETS - max_exact)).astype(np.int64)
    large = np.minimum(large, REL_BUCKETS - 1)
    return np.where(n < max_exact, n, large).astype(np.int32)


def _bias_kernel(tab_ref, idx_tile_ref, idx_cmp_ref, tile_ref, cmp_ref):
    h = pl.program_id(0) * HEADS_PER_GROUP + pl.program_id(1)
    for idx_ref, out_ref in ((idx_tile_ref, tile_ref), (idx_cmp_ref, cmp_ref)):
        idx = idx_ref[...]
        acc = jnp.full(idx.shape, NEG_INF, F32)
        for b in range(REL_BUCKETS):
            acc = jnp.where(idx == b, tab_ref[b, h] * LOG2_E, acc)
        out_ref[...] = acc


N_BIAS_TILES = 7
MASKED_TILE = N_BIAS_TILES - 1


def _bias_tables(rel_bias, seq):
    t = ATT_TILE
    kt = KEY_TILE
    n_q = seq // t
    r = np.arange(t)[None, :]
    masked = REL_BUCKETS

    def tile(offset, windowed):
        d = offset + r - np.arange(kt)[:, None]
        ok = (d >= 0) & ((d < WINDOW) | (not windowed))
        return np.where(ok, _bucket_of_distance(np.maximum(d, 0)), masked)

    idx_tile = np.stack([tile(0, False), tile(t, False), tile(2 * t, False), tile(3 * t, False),
                         tile(4 * t, True), tile(5 * t, True),
                         np.full((kt, t), masked)]).astype(np.int32)
    assert idx_tile.shape[0] == N_BIAS_TILES and (idx_tile[3] == REL_BUCKETS - 1).all()
    assert (tile(3 * t, True) == idx_tile[3]).all()
    cmp_end = np.arange(t)[:, None] * CMP_STRIDE + CMP_BLOCK - 1
    idx_cmp = np.stack([_bucket_of_distance(np.maximum(i * t + r - cmp_end, 0))
                        for i in range(n_q)]).astype(np.int32)
    return pl.pallas_call(
        _bias_kernel,
        grid=(KV_GROUPS, HEADS_PER_GROUP),
        in_specs=[
            pl.BlockSpec(memory_space=pltpu.SMEM),
            pl.BlockSpec((N_BIAS_TILES, kt, t), lambda g, j: (0, 0, 0)),
            pl.BlockSpec((n_q, t, t), lambda g, j: (0, 0, 0)),
        ],
        out_specs=[
            pl.BlockSpec((None, N_BIAS_TILES, kt, t), lambda g, j: (g, 0, 0, j)),
            pl.BlockSpec((None, n_q, t, t), lambda g, j: (g, 0, 0, j)),
        ],
        out_shape=[
            jax.ShapeDtypeStruct((KV_GROUPS, N_BIAS_TILES, kt, HEADS_PER_GROUP * t), F32),
            jax.ShapeDtypeStruct((KV_GROUPS, n_q, t, HEADS_PER_GROUP * t), F32),
        ],
        compiler_params=_params("parallel", "parallel"),
        name="bias_tables",
    )(rel_bias, jnp.asarray(idx_tile), jnp.asarray(idx_cmp))


def _attn_kernel(q_ref, ks_ref, vs_ref, kw_ref, vw_ref, kc_ref, vc_ref, gsel_ref, bgate_ref,
                 btile_ref, bcmp_ref, ov_ref, expand_ref, eye_ref, o_ref,
                 vst_ref, vwt_ref, s_ref, osel_ref):
    t = ATT_TILE
    kt = KEY_TILE
    hg = HEADS_PER_GROUP
    cols = hg * t
    n_kt = s_ref.shape[0]
    n_q = n_kt * kt // t
    i = pl.program_id(1)
    last_tile = lax.shift_right_logical(i, 1)
    eye = eye_ref[...]

    @pl.when(i == 0)
    def _():
        for jt in range(n_kt):
            ks = slice(jt * kt, (jt + 1) * kt)
            vst_ref[jt] = _dot_nt(eye, vs_ref[ks, :]).astype(BF16)
            vwt_ref[jt] = _dot_nt(eye, vw_ref[ks, :]).astype(BF16)

    qb = q_ref[...]
    qs = jnp.concatenate([qb[:, j * HEAD_DIM:(j + 1) * HEAD_DIM] for j in range(hg)], axis=0)

    n_idx = lax.broadcasted_iota(jnp.int32, (t, cols), 0)
    tq = i * t + (lax.broadcasted_iota(jnp.int32, (t, cols), 1) & (t - 1))
    valid = tq >= n_idx * CMP_STRIDE + (CMP_BLOCK - 1)
    s = jnp.where(valid, _dot_nt(kc_ref[...], qs) + bcmp_ref[...], NEG_INF)
    e = jnp.where(valid, jnp.exp2(s - jnp.max(s, axis=0, keepdims=True)), 0.0)
    l = jnp.sum(e, axis=0, keepdims=True)
    p_c = e * (1.0 / jnp.where(l > 0.0, l, 1.0))
    vct = _dot_nt(eye, vc_ref[...]).astype(BF16)
    ocmp_t = _dot(vct, p_c.astype(BF16))

    psum = p_c[:, 0:t]
    for j in range(1, hg):
        psum = psum + p_c[:, j * t:(j + 1) * t]
    p_hi = psum.astype(BF16)
    p_lo = (psum - p_hi.astype(F32)).astype(BF16)
    ov = ov_ref[...]
    n_sel = n_kt * kt // SEL_BLOCK
    imp = (_dot(ov, p_hi) + _dot(ov, p_lo))[0:n_sel]
    blk = lax.broadcasted_iota(jnp.int32, (n_sel, t), 0)
    tqb = i * t + lax.broadcasted_iota(jnp.int32, (n_sel, t), 1)
    cur = lax.shift_right_logical(tqb, int(math.log2(SEL_BLOCK)))
    forced = (blk == 0) | (blk == cur) | (blk == cur - 1)
    future = blk * SEL_BLOCK > tqb
    imp = jnp.where(future, -1.0, imp + jnp.where(forced, FORCE_BONUS, 0.0))
    rank = jnp.zeros((n_sel, t), F32)
    for sp in range(n_sel):
        other = imp[sp:sp + 1, :]
        beats = (other > imp) | ((other == imp) & (blk > sp))
        rank = rank + jnp.where(beats, 1.0, 0.0)
    unsel = jnp.where(rank < float(min(N_SELECT, n_sel)), 0.0, 1.0)
    unsel_pad = jnp.concatenate([unsel, jnp.zeros((t - n_sel, t), F32)], axis=0).astype(BF16)

    def bias_index(delta):
        return jnp.where(delta < 0, MASKED_TILE, delta)

    def attend(k_tile, bias_tile, vt_tile, n_sub):
        mx = jnp.full((kt, cols), NEG_INF, F32)
        for js in range(n_sub):
            st = _dot_nt(k_tile(js), qs) + bias_tile(js)
            s_ref[js] = st
            mx = jnp.maximum(mx, st)
        m = jnp.max(mx, axis=0, keepdims=True)
        lsum = jnp.zeros((kt, cols), F32)
        acc = jnp.zeros((HEAD_DIM, cols), F32)
        for js in range(n_sub):
            p = jnp.exp2(s_ref[js] - m)
            lsum = lsum + p
            acc = acc + _dot(vt_tile(js), p.astype(BF16))
        return acc * (1.0 / jnp.sum(lsum, axis=0, keepdims=True))

    def sel_bias(js):
        msk = _dot(expand_ref[js], unsel_pad)
        return (btile_ref[bias_index(jnp.minimum(i - 2 * js, 3))]
                + jnp.concatenate([msk] * hg, axis=1))

    tiles_per_variant = 2
    q_per_variant = tiles_per_variant * kt // t
    for variant in range(n_q // q_per_variant):
        @pl.when(i // q_per_variant == variant)
        def _(variant=variant):
            osel_ref[...] = attend(lambda js: ks_ref[js * kt:(js + 1) * kt, :], sel_bias,
                                   lambda js: vst_ref[js], tiles_per_variant * (variant + 1))

    win_tiles = WINDOW // kt + 1
    first = jnp.maximum(last_tile - (win_tiles - 1), 0)

    def win_k(js):
        return kw_ref[pl.ds(pl.multiple_of((first + js) * kt, kt), kt), :]

    owin_t = attend(win_k, lambda js: btile_ref[bias_index(i - 2 * (first + js))],
                    lambda js: vwt_ref[first + js], win_tiles)
    osel_t = osel_ref[...]

    gt = jnp.transpose(jax.nn.sigmoid(gsel_ref[...]))
    for j in range(hg):
        cs = slice(j * t, (j + 1) * t)
        ot = (gt[j:j + 1] * ocmp_t[:, cs] + gt[hg + j:hg + j + 1] * osel_t[:, cs]
              + gt[2 * hg + j:2 * hg + j + 1] * owin_t[:, cs])
        hs = slice(j * HEAD_DIM, (j + 1) * HEAD_DIM)
        o_ref[:, hs] = (jnp.transpose(ot) * jax.nn.silu(bgate_ref[:, hs])).astype(o_ref.dtype)


def _attention_constants(seq):
    t = ATT_TILE
    n_cmp_pad = seq // CMP_STRIDE
    assert n_cmp_pad == t, "one tile of compressed keys per sequence"
    n_cmp = n_cmp_pad - CMP_BLOCK // CMP_STRIDE + 1
    n_sel = seq // SEL_BLOCK
    cst = np.arange(n_cmp_pad) * CMP_STRIDE
    sst = np.arange(n_sel) * SEL_BLOCK
    overlap = ((cst[:, None] < sst[None] + SEL_BLOCK) & (cst[:, None] + CMP_BLOCK > sst[None]))
    overlap[n_cmp:] = False
    ov = np.zeros((t, n_cmp_pad), np.float32)
    ov[:n_sel] = overlap.T
    keys = np.arange(seq)
    expand = np.zeros((seq // KEY_TILE, KEY_TILE, t), np.float32)
    expand[keys // KEY_TILE, keys % KEY_TILE, keys // SEL_BLOCK] = NEG_INF
    eye = np.eye(t, dtype=np.float32)
    return (jnp.asarray(ov, BF16), jnp.asarray(expand, BF16), jnp.asarray(eye, BF16))


def _attention(zh, zf, kvc, bias_tile, bias_cmp, consts, bsz, seq):
    t = ATT_TILE
    kt = KEY_TILE
    n_q = seq // t
    n_kt = seq // kt
    gw = HEADS_PER_GROUP * HEAD_DIM
    kv0 = D_MODEL // HEAD_DIM
    ov, expand, eye = consts

    def kv_spec(kind):
        return pl.BlockSpec((seq, HEAD_DIM),
                            lambda bg, i: (bg // KV_GROUPS, kv0 + kind * KV_GROUPS + bg % KV_GROUPS))

    def cmp_spec(kind):
        return pl.BlockSpec((None, None, None, t, HEAD_DIM),
                            lambda bg, i: (bg // KV_GROUPS, kind, bg % KV_GROUPS, 0, 0))

    def tok(bg, i):
        return (bg // KV_GROUPS) * n_q + i

    return pl.pallas_call(
        _attn_kernel,
        grid=(bsz * KV_GROUPS, n_q),
        in_specs=[
            pl.BlockSpec((t, gw), lambda bg, i: (tok(bg, i), bg % KV_GROUPS)),
            kv_spec(0), kv_spec(1), kv_spec(2), kv_spec(3),
            cmp_spec(0), cmp_spec(1),
            pl.BlockSpec((t, V7X_LANES), lambda bg, i: (tok(bg, i), ZF_GSEL // V7X_LANES + bg % KV_GROUPS)),
            pl.BlockSpec((t, gw), lambda bg, i: (tok(bg, i), ZF_BG // gw + bg % KV_GROUPS)),
            pl.BlockSpec((None, N_BIAS_TILES, kt, gw), lambda bg, i: (bg % KV_GROUPS, 0, 0, 0)),
            pl.BlockSpec((None, None, t, gw), lambda bg, i: (bg % KV_GROUPS, i, 0, 0)),
            pl.BlockSpec((t, t), lambda bg, i: (0, 0)),
            pl.BlockSpec((n_kt, kt, t), lambda bg, i: (0, 0, 0)),
            pl.BlockSpec((t, t), lambda bg, i: (0, 0)),
        ],
        out_specs=pl.BlockSpec((t, gw), lambda bg, i: (tok(bg, i), bg % KV_GROUPS)),
        out_shape=jax.ShapeDtypeStruct((bsz * seq, D_MODEL), BF16),
        scratch_shapes=[pltpu.VMEM((n_kt, HEAD_DIM, kt), BF16),
                        pltpu.VMEM((n_kt, HEAD_DIM, kt), BF16),
                        pltpu.VMEM((n_kt, kt, gw), F32),
                        pltpu.VMEM((HEAD_DIM, gw), F32)],
        compiler_params=_params("parallel", "arbitrary"),
        name="nsa_attention",
    )(zh, zh, zh, zh, zh, kvc, kvc, zf, zf, bias_tile, bias_cmp, ov, expand, eye)


def _merge_kernel(ya_ref, yb_ref, yc_ref, ma_ref, mb_ref, mc_ref, w_ref, o_ref):
    y = jax.nn.sigmoid(ma_ref[...]) * _dot(ya_ref[...], w_ref[0])
    y += jax.nn.sigmoid(mb_ref[...]) * _dot(yb_ref[...], w_ref[1])
    y += jax.nn.sigmoid(mc_ref[...]) * _dot(yc_ref[...], w_ref[2])
    o_ref[...] = y.astype(o_ref.dtype)


def _merge(ya, yb, yc, zf, w_branch):
    m, d = ya.shape
    tm, tn = 512, 512
    m0 = ZF_MERGE // tn
    per = d // tn
    y_spec = pl.BlockSpec((tm, d), lambda i, j: (i, 0))
    return pl.pallas_call(
        _merge_kernel,
        grid=(m // tm, d // tn),
        in_specs=[
            y_spec, y_spec, y_spec,
            pl.BlockSpec((tm, tn), lambda i, j: (i, m0 + j)),
            pl.BlockSpec((tm, tn), lambda i, j: (i, m0 + per + j)),
            pl.BlockSpec((tm, tn), lambda i, j: (i, m0 + 2 * per + j)),
            pl.BlockSpec((3, d, tn), lambda i, j: (0, 0, j)),
        ],
        out_specs=pl.BlockSpec((tm, tn), lambda i, j: (i, j)),
        out_shape=jax.ShapeDtypeStruct((m, d), BF16),
        compiler_params=_params("parallel", "arbitrary"),
        name="merge",
    )(ya, yb, yc, zf, zf, zf, w_branch)


def _out_kernel(y_ref, w_ref, x_ref, mod_ref, o_ref):
    o_ref[...] = x_ref[...] + mod_ref[2:3, :] * _dot(y_ref[...], w_ref[...])


def _out_proj(y, w_out, x2, mod3, seq):
    m, d = y.shape
    tm, tn = 512, 512
    per_seq = seq // tm
    return pl.pallas_call(
        _out_kernel,
        grid=(m // tm, d // tn),
        in_specs=[
            pl.BlockSpec((tm, d), lambda i, j: (i, 0)),
            pl.BlockSpec((d, tn), lambda i, j: (0, j)),
            pl.BlockSpec((tm, tn), lambda i, j: (i, j)),
            pl.BlockSpec((None, 3, tn), lambda i, j: (i // per_seq, 0, j)),
        ],
        out_specs=pl.BlockSpec((tm, tn), lambda i, j: (i, j)),
        out_shape=jax.ShapeDtypeStruct((m, d), F32),
        compiler_params=_params("parallel", "arbitrary"),
        name="out_proj",
    )(y, w_out, x2, mod3)


def _final_norm_kernel(x_ref, g_ref, o_ref):
    x = x_ref[...]
    o_ref[...] = x * lax.rsqrt(jnp.mean(x * x, axis=-1, keepdims=True) + EPS) * g_ref[...]


def _final_norm(x2, g):
    m, d = x2.shape
    tm = 512
    return pl.pallas_call(
        _final_norm_kernel,
        grid=(m // tm,),
        in_specs=[pl.BlockSpec((tm, d), lambda i: (i, 0)), pl.BlockSpec((1, d), lambda i: (0, 0))],
        out_specs=pl.BlockSpec((tm, d), lambda i: (i, 0)),
        out_shape=jax.ShapeDtypeStruct((m, d), F32),
        compiler_params=_params("parallel"),
        name="final_norm",
    )(x2, g.reshape(1, d))


def _split_w_in(w):
    d = D_MODEL
    kvw = KV_GROUPS * HEAD_DIM
    sizes = (d, d, d, d, 6 * kvw, 3 * N_HEADS, d, d, d, 3 * d)
    offs = np.concatenate([[0], np.cumsum(sizes)])
    a_u, a_v, a_g, q, kv, gsel, b_g, c_x, c_g, merge = [w[:, offs[k]:offs[k + 1]] for k in range(10)]
    gs = gsel.reshape(d, 3, KV_GROUPS, HEADS_PER_GROUP).transpose(0, 2, 1, 3)
    gs = gs.reshape(d, KV_GROUPS, 3 * HEADS_PER_GROUP)
    gs = jnp.pad(gs, ((0, 0), (0, 0), (0, V7X_LANES - 3 * HEADS_PER_GROUP))).reshape(d, KV_GROUPS * V7X_LANES)
    wf = jnp.concatenate([a_u, a_v, a_g, b_g, c_x, c_g, merge, kv[:, :2 * kvw], gs], axis=1)
    wh = jnp.concatenate([q, kv[:, 2 * kvw:]], axis=1)
    assert wf.shape[1] == ZF_COLS and wh.shape[1] == ZH_COLS
    return wf.astype(BF16), wh.astype(BF16)


def kernel(x, c, rel_bias, norm_g, w_ada, b_ada, w_in, a_ln_g, a_ln_b, a_w_s, a_b_s, b_w_cmp1,
           b_w_cmp2, b_pos_cmp, c_w_grp, c_scale, w_branch, w_out, final_g):
    bsz, seq, d = x.shape
    depth = w_in.shape[0]
    assert d == D_MODEL and seq % 512 == 0
    x2 = x.reshape(bsz * seq, d)
    mod = _ada(c, w_ada, b_ada)
    bias_tile, bias_cmp = _bias_tables(rel_bias, seq)
    consts = _attention_constants(seq)
    q_blocks = D_MODEL // 512
    for l in range(depth):
        mod3 = mod[l].reshape(bsz, 3, d)
        wf, wh = _split_w_in(w_in[l])
        h = _prenorm(x2, norm_g[l], mod3, seq)
        zf = _proj(h, wf, F32)
        zh = _proj(h, wh, BF16, scale=HEAD_DIM ** -0.5 * LOG2_E, n_scaled=q_blocks)
        ya = _sgu(zf, a_ln_g[l], a_ln_b[l], a_w_s[l], a_b_s[l])
        kvc = _compress(zf, b_w_cmp1[l], b_w_cmp2[l], b_pos_cmp[l], bsz, seq)
        yb = _attention(zh, zf, kvc, bias_tile, bias_cmp, consts, bsz, seq)
        yc = _pool(zf, c_w_grp[l], c_scale[l], seq)
        y = _merge(ya, yb, yc, zf, w_branch[l].astype(BF16))
        x2 = _out_proj(y, w_out[l].astype(BF16), x2, mod3, seq)
    return _final_norm(x2, final_g).reshape(bsz, seq, d)
```

```python
import functools
import math

import numpy as np
import jax
import jax.numpy as jnp
from jax import lax
from jax.experimental import pallas as pl
from jax.experimental.pallas import tpu as pltpu

D_MODEL = 2048
A_GROUPS = 8
A_CHUNK = 128
HEAD_DIM = 128
N_HEADS = 16
KV_GROUPS = 4
HEADS_PER_GROUP = N_HEADS // KV_GROUPS
CMP_BLOCK = 32
CMP_STRIDE = 16
SEL_BLOCK = 64
N_SELECT = 16
WINDOW = 512
FORCE_BONUS = 1e4
POOL_WINDOWS = (2, 4, 8, 16)
C_GROUP = D_MODEL // len(POOL_WINDOWS)
REL_BUCKETS = 32
REL_MAX_DIST = 128
EPS = 1e-6
NEG_INF = -1e30

V7X_LANES = 128
V7X_VMEM_BYTES = 64 * 1024 * 1024
VMEM_LIMIT = 48 * 1024 * 1024

ATT_TILE = 128
KEY_TILE = 256
LOG2_E = math.log2(math.e)
POOL_HALO = 16
BF16 = jnp.bfloat16
F32 = jnp.float32

Z_Q = 0
Z_KV = 2048
Z_U, Z_V, Z_AG = 4096, 6144, 8192
Z_BG = 10240
Z_CX, Z_CG = 12288, 14336
Z_MERGE = 16384
Z_GSEL = 22528
Z_COLS = 23040
ZC_COLS = 1024


def _dot(a, b):
    return jnp.dot(a, b, preferred_element_type=F32)


def _dot_nt(a, b):
    return lax.dot_general(a, b, (((1,), (1,)), ((), ())), preferred_element_type=F32)


def _params(*sem):
    return pltpu.CompilerParams(dimension_semantics=sem, vmem_limit_bytes=VMEM_LIMIT)


def _ada_kernel(c_ref, w_ref, b_ref, o_ref):
    a = jax.nn.silu(c_ref[...]).astype(BF16)
    o_ref[...] = _dot(a, w_ref[...].astype(BF16)) + b_ref[...]


def _ada(c, w_ada, b_ada):
    depth, d, n = w_ada.shape
    bsz = c.shape[0]
    tn = 512
    return pl.pallas_call(
        _ada_kernel,
        grid=(depth, n // tn),
        in_specs=[
            pl.BlockSpec((bsz, d), lambda l, j: (0, 0)),
            pl.BlockSpec((None, d, tn), lambda l, j: (l, 0, j)),
            pl.BlockSpec((None, 1, tn), lambda l, j: (l, 0, j)),
        ],
        out_specs=pl.BlockSpec((None, bsz, tn), lambda l, j: (l, 0, j)),
        out_shape=jax.ShapeDtypeStruct((depth, bsz, n), F32),
        compiler_params=_params("parallel", "parallel"),
        name="ada_mod",
    )(c, w_ada, b_ada.reshape(depth, 1, n))


def _prenorm_kernel(x_ref, g_ref, mod_ref, o_ref):
    x = x_ref[...]
    y = x * lax.rsqrt(jnp.mean(x * x, axis=-1, keepdims=True) + EPS) * g_ref[...]
    m = mod_ref[...]
    o_ref[...] = (y * (1.0 + m[1:2]) + m[0:1]).astype(o_ref.dtype)


def _prenorm(x2, g, mod3, seq):
    m, d = x2.shape
    tm = 512
    per_seq = seq // tm
    return pl.pallas_call(
        _prenorm_kernel,
        grid=(m // tm,),
        in_specs=[
            pl.BlockSpec((tm, d), lambda i: (i, 0)),
            pl.BlockSpec((1, d), lambda i: (0, 0)),
            pl.BlockSpec((None, 3, d), lambda i: (i // per_seq, 0, 0)),
        ],
        out_specs=pl.BlockSpec((tm, d), lambda i: (i, 0)),
        out_shape=jax.ShapeDtypeStruct((m, d), BF16),
        compiler_params=_params("parallel"),
        name="prenorm",
    )(x2, g.reshape(1, d), mod3)


def _proj_kernel(a_ref, w_ref, o_ref, *, scale, n_scaled):
    acc = _dot(a_ref[...], w_ref[...])
    if n_scaled:
        acc = acc * jnp.where(pl.program_id(1) < n_scaled, scale, 1.0).astype(F32)
    o_ref[...] = acc.astype(o_ref.dtype)


def _proj(a, w, out_dtype, scale=1.0, n_scaled=0, tm=1024, tn=512):
    m, k = a.shape
    n = w.shape[1]
    tm = min(tm, m)
    return pl.pallas_call(
        functools.partial(_proj_kernel, scale=scale, n_scaled=n_scaled),
        grid=(m // tm, n // tn),
        in_specs=[
            pl.BlockSpec((tm, k), lambda i, j: (i, 0)),
            pl.BlockSpec((k, tn), lambda i, j: (0, j)),
        ],
        out_specs=pl.BlockSpec((tm, tn), lambda i, j: (i, j)),
        out_shape=jax.ShapeDtypeStruct((m, n), out_dtype),
        compiler_params=_params("parallel", "arbitrary"),
        name="in_proj",
    )(a, w)


def _sgu_kernel(u_ref, v_ref, gate_ref, lng_ref, lnb_ref, ws_ref, bst_ref, o_ref):
    tm = u_ref.shape[0]
    cg = D_MODEL // A_GROUPS
    v = jax.nn.gelu(v_ref[...].astype(F32))
    mu = jnp.mean(v, axis=-1, keepdims=True)
    vc = v - mu
    var = jnp.mean(vc * vc, axis=-1, keepdims=True)
    vn = (vc * lax.rsqrt(var + EPS) * lng_ref[...] + lnb_ref[...]).astype(BF16)
    row = lax.broadcasted_iota(jnp.int32, (A_CHUNK, A_CHUNK), 0)
    col = lax.broadcasted_iota(jnp.int32, (A_CHUNK, A_CHUNK), 1)
    tril = row >= col
    bst = bst_ref[...]
    for g in range(A_GROUPS):
        ws = jnp.where(tril, ws_ref[g], 0.0).astype(BF16)
        cs = slice(g * cg, (g + 1) * cg)
        for c in range(tm // A_CHUNK):
            rs = slice(c * A_CHUNK, (c + 1) * A_CHUNK)
            mixed = _dot(ws, vn[rs, cs]) + bst[:, g:g + 1]
            y = (jax.nn.gelu(u_ref[rs, cs].astype(F32)) * mixed
                 * jax.nn.silu(gate_ref[rs, cs].astype(F32)))
            o_ref[rs, cs] = y.astype(o_ref.dtype)


def _sgu(zf, ln_g, ln_b, w_s, b_s):
    m = zf.shape[0]
    tm = 256
    d = D_MODEL
    return pl.pallas_call(
        _sgu_kernel,
        grid=(m // tm,),
        in_specs=[
            pl.BlockSpec((tm, d), lambda i: (i, Z_U // d)),
            pl.BlockSpec((tm, d), lambda i: (i, Z_V // d)),
            pl.BlockSpec((tm, d), lambda i: (i, Z_AG // d)),
            pl.BlockSpec((1, d), lambda i: (0, 0)),
            pl.BlockSpec((1, d), lambda i: (0, 0)),
            pl.BlockSpec((A_GROUPS, A_CHUNK, A_CHUNK), lambda i: (0, 0, 0)),
            pl.BlockSpec((A_CHUNK, A_GROUPS), lambda i: (0, 0)),
        ],
        out_specs=pl.BlockSpec((tm, d), lambda i: (i, 0)),
        out_shape=jax.ShapeDtypeStruct((m, d), BF16),
        compiler_params=_params("parallel"),
        name="sgu",
    )(zf, zf, zf, ln_g.reshape(1, d), ln_b.reshape(1, d), w_s, b_s.T)


def _pool_kernel(x_ref, halo_ref, gate_ref, wg_ref, ls_ref, o_ref, ext_ref, *, tiles_per_seq):
    tm = x_ref.shape[0]
    i = pl.program_id(0)
    pos0 = (i % tiles_per_seq) * tm
    halo = halo_ref[...].astype(F32)
    ext_ref[0:POOL_HALO, :] = jnp.where(pos0 == 0, jnp.zeros_like(halo), halo)
    ext_ref[POOL_HALO:POOL_HALO + tm, :] = x_ref[...].astype(F32)
    tpos = pos0 + lax.broadcasted_iota(jnp.int32, (tm, 1), 0)
    for gi, w in enumerate(POOL_WINDOWS):
        cs = slice(gi * C_GROUP, (gi + 1) * C_GROUP)
        x = ext_ref[POOL_HALO:POOL_HALO + tm, cs]
        s = x
        for back in range(1, w):
            s = s + ext_ref[POOL_HALO - back:POOL_HALO - back + tm, cs]
        cnt = jnp.minimum(tpos + 1, w).astype(F32)
        y = (s / cnt - x).astype(BF16)
        yg = (_dot(y, wg_ref[gi].astype(BF16)) * ls_ref[:, cs]
              * jax.nn.silu(gate_ref[:, cs].astype(F32)))
        o_ref[:, cs] = yg.astype(o_ref.dtype)


def _pool(zf, w_grp, ls, seq):
    m = zf.shape[0]
    tm = 256
    d = D_MODEL
    halo_blocks = tm // POOL_HALO
    return pl.pallas_call(
        functools.partial(_pool_kernel, tiles_per_seq=seq // tm),
        grid=(m // tm,),
        in_specs=[
            pl.BlockSpec((tm, d), lambda i: (i, Z_CX // d)),
            pl.BlockSpec((POOL_HALO, d), lambda i: (jnp.maximum(i * halo_blocks - 1, 0), Z_CX // d)),
            pl.BlockSpec((tm, d), lambda i: (i, Z_CG // d)),
            pl.BlockSpec((len(POOL_WINDOWS), C_GROUP, C_GROUP), lambda i: (0, 0, 0)),
            pl.BlockSpec((1, d), lambda i: (0, 0)),
        ],
        out_specs=pl.BlockSpec((tm, d), lambda i: (i, 0)),
        out_shape=jax.ShapeDtypeStruct((m, d), BF16),
        scratch_shapes=[pltpu.VMEM((POOL_HALO + tm, d), F32)],
        compiler_params=_params("parallel"),
        name="pool",
    )(zf, zf, zf, w_grp, ls.reshape(1, d))


def _compress_kernel(x_ref, w1_ref, w2_ref, pos_ref, o_ref):
    n_chunks = x_ref.shape[0] // CMP_STRIDE
    pos = pos_ref[...]
    first = jnp.zeros((n_chunks, HEAD_DIM), F32)
    second = jnp.zeros((n_chunks, HEAD_DIM), F32)
    for l in range(CMP_STRIDE):
        xl = x_ref[pl.ds(l, n_chunks, stride=CMP_STRIDE), :]
        first += _dot((xl + pos[l:l + 1]).astype(BF16), w1_ref[l].astype(BF16))
        second += _dot((xl + pos[CMP_STRIDE + l:CMP_STRIDE + l + 1]).astype(BF16),
                       w1_ref[CMP_STRIDE + l].astype(BF16))
    hdn = jax.nn.gelu(first + pltpu.roll(second, n_chunks - 1, axis=0))
    o_ref[...] = _dot(hdn.astype(BF16), w2_ref[...].astype(BF16)).astype(o_ref.dtype)


def _compress(zc, w1, w2, pos, bsz, seq):
    n_chunks = seq // CMP_STRIDE
    return pl.pallas_call(
        _compress_kernel,
        grid=(bsz, 2, KV_GROUPS),
        in_specs=[
            pl.BlockSpec((seq, HEAD_DIM), lambda b, t, g: (b, t * KV_GROUPS + g)),
            pl.BlockSpec((None, CMP_BLOCK, HEAD_DIM, HEAD_DIM), lambda b, t, g: (t, 0, 0, 0)),
            pl.BlockSpec((None, HEAD_DIM, HEAD_DIM), lambda b, t, g: (t, 0, 0)),
            pl.BlockSpec((None, CMP_BLOCK, HEAD_DIM), lambda b, t, g: (t, 0, 0)),
        ],
        out_specs=pl.BlockSpec((None, None, None, n_chunks, HEAD_DIM), lambda b, t, g: (b, t, g, 0, 0)),
        out_shape=jax.ShapeDtypeStruct((bsz, 2, KV_GROUPS, n_chunks, HEAD_DIM), BF16),
        compiler_params=_params("parallel", "parallel", "parallel"),
        name="compress",
    )(zc, w1, w2, pos)


def _bucket_of_distance(n):
    n = np.asarray(n, np.int64)
    max_exact = REL_BUCKETS // 2
    nf = np.maximum(n, 1).astype(np.float64)
    large = max_exact + (np.log(nf / max_exact) / math.log(REL_MAX_DIST / max_exact)
                         * (REL_BUCKETS - max_exact)).astype(np.int64)
    large = np.minimum(large, REL_BUCKETS - 1)
    return np.where(n < max_exact, n, large).astype(np.int32)


def _bias_kernel(tab_ref, idx_tile_ref, idx_cmp_ref, tile_ref, cmp_ref):
    h = pl.program_id(0) * HEADS_PER_GROUP + pl.program_id(1)
    far = tab_ref[REL_BUCKETS - 1, h]
    for idx_ref, out_ref, shift in ((idx_tile_ref, tile_ref, far), (idx_cmp_ref, cmp_ref, 0.0)):
        idx = idx_ref[...]
        acc = jnp.full(idx.shape, NEG_INF, F32)
        for b in range(REL_BUCKETS):
            acc = jnp.where(idx == b, (tab_ref[b, h] - shift) * LOG2_E, acc)
        out_ref[...] = acc


N_BIAS_TILES = 7
MASKED_TILE = N_BIAS_TILES - 1


def _bias_tables(rel_bias, seq):
    t = ATT_TILE
    kt = KEY_TILE
    n_q = seq // t
    r = np.arange(t)[None, :]
    masked = REL_BUCKETS

    def tile(offset, windowed):
        d = offset + r - np.arange(kt)[:, None]
        ok = (d >= 0) & ((d < WINDOW) | (not windowed))
        return np.where(ok, _bucket_of_distance(np.maximum(d, 0)), masked)

    idx_tile = np.stack([tile(0, False), tile(t, False), tile(2 * t, False), tile(3 * t, False),
                         tile(4 * t, True), tile(5 * t, True),
                         np.full((kt, t), masked)]).astype(np.int32)
    assert idx_tile.shape[0] == N_BIAS_TILES and (idx_tile[3] == REL_BUCKETS - 1).all()
    assert (tile(3 * t, True) == idx_tile[3]).all()
    cmp_end = np.arange(t)[:, None] * CMP_STRIDE + CMP_BLOCK - 1
    idx_cmp = np.stack([_bucket_of_distance(np.maximum(i * t + r - cmp_end, 0))
                        for i in range(n_q)]).astype(np.int32)
    return pl.pallas_call(
        _bias_kernel,
        grid=(KV_GROUPS, HEADS_PER_GROUP),
        in_specs=[
            pl.BlockSpec(memory_space=pltpu.SMEM),
            pl.BlockSpec((N_BIAS_TILES, kt, t), lambda g, j: (0, 0, 0)),
            pl.BlockSpec((n_q, t, t), lambda g, j: (0, 0, 0)),
        ],
        out_specs=[
            pl.BlockSpec((None, N_BIAS_TILES, kt, t), lambda g, j: (g, 0, 0, j)),
            pl.BlockSpec((None, n_q, t, t), lambda g, j: (g, 0, 0, j)),
        ],
        out_shape=[
            jax.ShapeDtypeStruct((KV_GROUPS, N_BIAS_TILES, kt, HEADS_PER_GROUP * t), F32),
            jax.ShapeDtypeStruct((KV_GROUPS, n_q, t, HEADS_PER_GROUP * t), F32),
        ],
        compiler_params=_params("parallel", "parallel"),
        name="bias_tables",
    )(rel_bias, jnp.asarray(idx_tile), jnp.asarray(idx_cmp))


def _attn_kernel(q_ref, ks_ref, vs_ref, kw_ref, vw_ref, kc_ref, vc_ref, gsel_ref, bgate_ref,
                 btile_ref, bcmp_ref, ov_ref, expand_ref, eye_ref, o_ref,
                 vst_ref, vwt_ref, kaug_ref, s_ref, w_ref, osel_ref):
    t = ATT_TILE
    kt = KEY_TILE
    hg = HEADS_PER_GROUP
    cols = hg * t
    n_kt = s_ref.shape[0]
    n_q = n_kt * kt // t
    i = pl.program_id(1)
    last_tile = lax.shift_right_logical(i, 1)
    eye = eye_ref[...]

    @pl.when(i == 0)
    def _():
        kaug_ref[:, 0:HEAD_DIM] = ks_ref[...]
        kaug_ref[:, HEAD_DIM:] = expand_ref[...]
        for jt in range(n_kt):
            ks = slice(jt * kt, (jt + 1) * kt)
            vst_ref[jt] = _dot_nt(eye, vs_ref[ks, :]).astype(BF16)
            vwt_ref[jt] = _dot_nt(eye, vw_ref[ks, :]).astype(BF16)

    qb = q_ref[...]
    qs = jnp.concatenate([qb[:, j * HEAD_DIM:(j + 1) * HEAD_DIM] for j in range(hg)], axis=0)

    n_idx = lax.broadcasted_iota(jnp.int32, (t, cols), 0)
    tq = i * t + (lax.broadcasted_iota(jnp.int32, (t, cols), 1) & (t - 1))
    valid = tq >= n_idx * CMP_STRIDE + (CMP_BLOCK - 1)
    s = jnp.where(valid, _dot_nt(kc_ref[...], qs) + bcmp_ref[...], NEG_INF)
    e = jnp.where(valid, jnp.exp2(s - jnp.max(s, axis=0, keepdims=True)), 0.0)
    l = jnp.sum(e, axis=0, keepdims=True)
    p_c = e * (1.0 / jnp.where(l > 0.0, l, 1.0))
    vct = _dot_nt(eye, vc_ref[...]).astype(BF16)
    ocmp_t = _dot(vct, p_c.astype(BF16))

    psum = p_c[:, 0:t]
    for j in range(1, hg):
        psum = psum + p_c[:, j * t:(j + 1) * t]
    p_hi = psum.astype(BF16)
    p_lo = (psum - p_hi.astype(F32)).astype(BF16)
    ov = ov_ref[...]
    n_sel = n_kt * kt // SEL_BLOCK
    imp = (_dot(ov, p_hi) + _dot(ov, p_lo))[0:n_sel]
    blk = lax.broadcasted_iota(jnp.int32, (n_sel, t), 0)
    tqb = i * t + lax.broadcasted_iota(jnp.int32, (n_sel, t), 1)
    cur = lax.shift_right_logical(tqb, int(math.log2(SEL_BLOCK)))
    forced = (blk == 0) | (blk == cur) | (blk == cur - 1)
    future = blk * SEL_BLOCK > tqb
    imp = jnp.where(future, -1.0, imp + jnp.where(forced, FORCE_BONUS, 0.0))
    rank = jnp.zeros((n_sel, t), F32)
    for sp in range(n_sel):
        other = imp[sp:sp + 1, :]
        beats = (other > imp) | ((other == imp) & (blk > sp))
        rank = rank + jnp.where(beats, 1.0, 0.0)
    unsel = jnp.where(rank < float(min(N_SELECT, n_sel)), 0.0, 1.0)
    unsel_pad = jnp.concatenate([unsel, jnp.zeros((t - n_sel, t), F32)], axis=0).astype(BF16)
    unsel_qb = _dot_nt(eye, unsel_pad).astype(BF16)
    q_aug = jnp.concatenate([qs, jnp.concatenate([unsel_qb] * hg, axis=0)], axis=1)

    def bias_index(delta):
        return jnp.where(delta < 0, MASKED_TILE, delta)

    def attend(q_op, k_tile, bias_tile, vt_tile, n_sub, sc_ref):
        mx = jnp.full((kt, cols), NEG_INF, F32)
        for js in range(n_sub):
            st = _dot_nt(k_tile(js), q_op)
            bias = bias_tile(js)
            if bias is not None:
                st = st + bias
            sc_ref[js] = st
            mx = jnp.maximum(mx, st)
        m = jnp.max(mx, axis=0, keepdims=True)
        lsum = jnp.zeros((kt, cols), F32)
        acc = jnp.zeros((HEAD_DIM, cols), F32)
        for js in range(n_sub):
            p = jnp.exp2(sc_ref[js] - m)
            lsum = lsum + p
            acc = acc + _dot(vt_tile(js), p.astype(BF16))
        return acc * (1.0 / jnp.sum(lsum, axis=0, keepdims=True))

    win_tiles = WINDOW // kt + 1
    first = jnp.maximum(last_tile - (win_tiles - 1), 0)

    def win_k(js):
        return kw_ref[pl.ds(pl.multiple_of((first + js) * kt, kt), kt), :]

    owin_t = attend(qs, win_k, lambda js: btile_ref[bias_index(i - 2 * (first + js))],
                    lambda js: vwt_ref[first + js], win_tiles, w_ref)

    tiles_per_variant = 2
    q_per_variant = tiles_per_variant * kt // t
    far_offset = 3
    for variant in range(n_q // q_per_variant):
        n_sub = tiles_per_variant * (variant + 1)
        n_far = max((q_per_variant * variant - far_offset) // 2 + 1, 0)

        def sel_bias(js, n_far=n_far):
            if js < n_far:
                return None
            return btile_ref[bias_index(jnp.minimum(i - 2 * js, far_offset))]

        @pl.when(i // q_per_variant == variant)
        def _(n_sub=n_sub, sel_bias=sel_bias):
            osel_ref[...] = attend(q_aug, lambda js: kaug_ref[js * kt:(js + 1) * kt, :], sel_bias,
                                   lambda js: vst_ref[js], n_sub, s_ref)

    osel_t = osel_ref[...]

    gt = jnp.transpose(jax.nn.sigmoid(gsel_ref[...].astype(F32)))
    for j in range(hg):
        cs = slice(j * t, (j + 1) * t)
        ot = (gt[j:j + 1] * ocmp_t[:, cs] + gt[hg + j:hg + j + 1] * osel_t[:, cs]
              + gt[2 * hg + j:2 * hg + j + 1] * owin_t[:, cs])
        hs = slice(j * HEAD_DIM, (j + 1) * HEAD_DIM)
        o_ref[:, hs] = (jnp.transpose(ot)
                        * jax.nn.silu(bgate_ref[:, hs].astype(F32))).astype(o_ref.dtype)


def _attention_constants(seq):
    t = ATT_TILE
    n_cmp_pad = seq // CMP_STRIDE
    assert n_cmp_pad == t, "one tile of compressed keys per sequence"
    n_cmp = n_cmp_pad - CMP_BLOCK // CMP_STRIDE + 1
    n_sel = seq // SEL_BLOCK
    cst = np.arange(n_cmp_pad) * CMP_STRIDE
    sst = np.arange(n_sel) * SEL_BLOCK
    overlap = ((cst[:, None] < sst[None] + SEL_BLOCK) & (cst[:, None] + CMP_BLOCK > sst[None]))
    overlap[n_cmp:] = False
    ov = np.zeros((t, n_cmp_pad), np.float32)
    ov[:n_sel] = overlap.T
    keys = np.arange(seq)
    expand = np.zeros((seq, t), np.float32)
    expand[keys, keys // SEL_BLOCK] = NEG_INF
    eye = np.eye(t, dtype=np.float32)
    return (jnp.asarray(ov, BF16), jnp.asarray(expand, BF16), jnp.asarray(eye, BF16))


def _attention(z, kvc, bias_tile, bias_cmp, consts, bsz, seq):
    t = ATT_TILE
    kt = KEY_TILE
    n_q = seq // t
    n_kt = seq // kt
    gw = HEADS_PER_GROUP * HEAD_DIM
    kv0 = Z_KV // HEAD_DIM
    ov, expand, eye = consts

    def kv_spec(kind):
        return pl.BlockSpec((seq, HEAD_DIM),
                            lambda bg, i: (bg // KV_GROUPS, kv0 + kind * KV_GROUPS + bg % KV_GROUPS))

    def cmp_spec(kind):
        return pl.BlockSpec((None, None, None, t, HEAD_DIM),
                            lambda bg, i: (bg // KV_GROUPS, kind, bg % KV_GROUPS, 0, 0))

    def tok(bg, i):
        return (bg // KV_GROUPS) * n_q + i

    return pl.pallas_call(
        _attn_kernel,
        grid=(bsz * KV_GROUPS, n_q),
        in_specs=[
            pl.BlockSpec((t, gw), lambda bg, i: (tok(bg, i), Z_Q // gw + bg % KV_GROUPS)),
            kv_spec(0), kv_spec(1), kv_spec(2), kv_spec(3),
            cmp_spec(0), cmp_spec(1),
            pl.BlockSpec((t, V7X_LANES), lambda bg, i: (tok(bg, i), Z_GSEL // V7X_LANES + bg % KV_GROUPS)),
            pl.BlockSpec((t, gw), lambda bg, i: (tok(bg, i), Z_BG // gw + bg % KV_GROUPS)),
            pl.BlockSpec((None, N_BIAS_TILES, kt, gw), lambda bg, i: (bg % KV_GROUPS, 0, 0, 0)),
            pl.BlockSpec((None, None, t, gw), lambda bg, i: (bg % KV_GROUPS, i, 0, 0)),
            pl.BlockSpec((t, t), lambda bg, i: (0, 0)),
            pl.BlockSpec((seq, t), lambda bg, i: (0, 0)),
            pl.BlockSpec((t, t), lambda bg, i: (0, 0)),
        ],
        out_specs=pl.BlockSpec((t, gw), lambda bg, i: (tok(bg, i), bg % KV_GROUPS)),
        out_shape=jax.ShapeDtypeStruct((bsz * seq, D_MODEL), BF16),
        scratch_shapes=[pltpu.VMEM((n_kt, HEAD_DIM, kt), BF16),
                        pltpu.VMEM((n_kt, HEAD_DIM, kt), BF16),
                        pltpu.VMEM((seq, HEAD_DIM + t), BF16),
                        pltpu.VMEM((n_kt, kt, gw), F32),
                        pltpu.VMEM((WINDOW // kt + 1, kt, gw), F32),
                        pltpu.VMEM((HEAD_DIM, gw), F32)],
        compiler_params=_params("parallel", "arbitrary"),
        name="nsa_attention",
    )(z, z, z, z, z, kvc, kvc, z, z, bias_tile, bias_cmp, ov, expand, eye)


def _merge_kernel(ya_ref, yb_ref, yc_ref, ma_ref, mb_ref, mc_ref, w_ref, o_ref):
    y = jax.nn.sigmoid(ma_ref[...].astype(F32)) * _dot(ya_ref[...], w_ref[0])
    y += jax.nn.sigmoid(mb_ref[...].astype(F32)) * _dot(yb_ref[...], w_ref[1])
    y += jax.nn.sigmoid(mc_ref[...].astype(F32)) * _dot(yc_ref[...], w_ref[2])
    o_ref[...] = y.astype(o_ref.dtype)


def _merge(ya, yb, yc, zf, w_branch):
    m, d = ya.shape
    tm, tn = min(1024, m), 256
    m0 = Z_MERGE // tn
    per = d // tn
    y_spec = pl.BlockSpec((tm, d), lambda i, j: (i, 0))
    return pl.pallas_call(
        _merge_kernel,
        grid=(m // tm, d // tn),
        in_specs=[
            y_spec, y_spec, y_spec,
            pl.BlockSpec((tm, tn), lambda i, j: (i, m0 + j)),
            pl.BlockSpec((tm, tn), lambda i, j: (i, m0 + per + j)),
            pl.BlockSpec((tm, tn), lambda i, j: (i, m0 + 2 * per + j)),
            pl.BlockSpec((3, d, tn), lambda i, j: (0, 0, j)),
        ],
        out_specs=pl.BlockSpec((tm, tn), lambda i, j: (i, j)),
        out_shape=jax.ShapeDtypeStruct((m, d), BF16),
        compiler_params=_params("parallel", "arbitrary"),
        name="merge",
    )(ya, yb, yc, zf, zf, zf, w_branch)


def _out_kernel(y_ref, w_ref, x_ref, mod_ref, g_ref, *rest, last):
    x = x_ref[...] + mod_ref[2:3, :] * _dot(y_ref[...], w_ref[...])
    normed = x * lax.rsqrt(jnp.mean(x * x, axis=-1, keepdims=True) + EPS) * g_ref[...]
    if last:
        (o_ref,) = rest
        o_ref[...] = normed
    else:
        next_mod_ref, x_out_ref, h_ref = rest
        nm = next_mod_ref[...]
        x_out_ref[...] = x
        h_ref[...] = (normed * (1.0 + nm[1:2]) + nm[0:1]).astype(h_ref.dtype)


def _out_proj(y, w_out, x2, mod3, g, next_mod3, seq):
    m, d = y.shape
    tm = 512
    per_seq = seq // tm
    last = next_mod3 is None
    row = pl.BlockSpec((tm, d), lambda i: (i, 0))
    mod_spec = pl.BlockSpec((None, 3, d), lambda i: (i // per_seq, 0, 0))
    in_specs = [row, pl.BlockSpec((d, d), lambda i: (0, 0)), row, mod_spec,
                pl.BlockSpec((1, d), lambda i: (0, 0))]
    args = [y, w_out, x2, mod3, g.reshape(1, d)]
    if last:
        out_specs = row
        out_shape = jax.ShapeDtypeStruct((m, d), F32)
    else:
        in_specs.append(mod_spec)
        args.append(next_mod3)
        out_specs = [row, row]
        out_shape = [jax.ShapeDtypeStruct((m, d), F32), jax.ShapeDtypeStruct((m, d), BF16)]
    return pl.pallas_call(
        functools.partial(_out_kernel, last=last),
        grid=(m // tm,),
        in_specs=in_specs,
        out_specs=out_specs,
        out_shape=out_shape,
        compiler_params=_params("parallel"),
        name="out_proj",
    )(*args)


def _split_w_in(w):
    d = D_MODEL
    kvw = KV_GROUPS * HEAD_DIM
    sizes = (d, d, d, d, 6 * kvw, 3 * N_HEADS, d, d, d, 3 * d)
    offs = np.concatenate([[0], np.cumsum(sizes)])
    a_u, a_v, a_g, q, kv, gsel, b_g, c_x, c_g, merge = [w[:, offs[k]:offs[k + 1]] for k in range(10)]
    gs = gsel.reshape(d, 3, KV_GROUPS, HEADS_PER_GROUP).transpose(0, 2, 1, 3)
    gs = gs.reshape(d, KV_GROUPS, 3 * HEADS_PER_GROUP)
    gs = jnp.pad(gs, ((0, 0), (0, 0), (0, V7X_LANES - 3 * HEADS_PER_GROUP))).reshape(d, KV_GROUPS * V7X_LANES)
    wz = jnp.concatenate([q, kv[:, 2 * kvw:], a_u, a_v, a_g, b_g, c_x, c_g, merge, gs], axis=1)
    wc = kv[:, :2 * kvw]
    assert wz.shape[1] == Z_COLS and wc.shape[1] == ZC_COLS
    return wz.astype(BF16), wc.astype(BF16)


def kernel(x, c, rel_bias, norm_g, w_ada, b_ada, w_in, a_ln_g, a_ln_b, a_w_s, a_b_s, b_w_cmp1,
           b_w_cmp2, b_pos_cmp, c_w_grp, c_scale, w_branch, w_out, final_g):
    bsz, seq, d = x.shape
    depth = w_in.shape[0]
    assert d == D_MODEL and seq % 512 == 0
    x2 = x.reshape(bsz * seq, d)
    mod = _ada(c, w_ada, b_ada)
    bias_tile, bias_cmp = _bias_tables(rel_bias, seq)
    consts = _attention_constants(seq)
    q_blocks = D_MODEL // 512
    mod3 = [mod[l].reshape(bsz, 3, d) for l in range(depth)]
    h = _prenorm(x2, norm_g[0], mod3[0], seq)
    for l in range(depth):
        wz, wc = _split_w_in(w_in[l])
        z = _proj(h, wz, BF16, scale=HEAD_DIM ** -0.5 * LOG2_E, n_scaled=q_blocks)
        zc = _proj(h, wc, F32)
        ya = _sgu(z, a_ln_g[l], a_ln_b[l], a_w_s[l], a_b_s[l])
        kvc = _compress(zc, b_w_cmp1[l], b_w_cmp2[l], b_pos_cmp[l], bsz, seq)
        yb = _attention(z, kvc, bias_tile, bias_cmp, consts, bsz, seq)
        yc = _pool(z, c_w_grp[l], c_scale[l], seq)
        y = _merge(ya, yb, yc, z, w_branch[l].astype(BF16))
        if l + 1 < depth:
            x2, h = _out_proj(y, w_out[l].astype(BF16), x2, mod3[l], norm_g[l + 1], mod3[l + 1], seq)
        else:
            out = _out_proj(y, w_out[l].astype(BF16), x2, mod3[l], final_g, None, seq)
    return out.reshape(bsz, seq, d)
```

```python
import functools
import math

import numpy as np
import jax
import jax.numpy as jnp
from jax import lax
from jax.experimental import pallas as pl
from jax.experimental.pallas import tpu as pltpu

D_MODEL = 2048
A_GROUPS = 8
A_CHUNK = 128
HEAD_DIM = 128
N_HEADS = 16
KV_GROUPS = 4
HEADS_PER_GROUP = N_HEADS // KV_GROUPS
CMP_BLOCK = 32
CMP_STRIDE = 16
SEL_BLOCK = 64
N_SELECT = 16
WINDOW = 512
FORCE_BONUS = 1e4
POOL_WINDOWS = (2, 4, 8, 16)
C_GROUP = D_MODEL // len(POOL_WINDOWS)
REL_BUCKETS = 32
REL_MAX_DIST = 128
EPS = 1e-6
NEG_INF = -1e30

V7X_LANES = 128
V7X_VMEM_BYTES = 64 * 1024 * 1024
VMEM_LIMIT = 48 * 1024 * 1024

ATT_TILE = 128
KEY_TILE = 256
LOG2_E = math.log2(math.e)
POOL_HALO = 16
BF16 = jnp.bfloat16
F32 = jnp.float32

Z_Q = 0
Z_CX, Z_CG = 2048, 4096
Z_BG = 6144
Z_MERGE = 8192
Z_KV = 14336
Z_GSEL = 17408
Z_COLS = 17920
KV_KINDS = ("k_c", "v_c", "k_s", "v_s", "k_w", "v_w")


def _dot(a, b):
    return jnp.dot(a, b, preferred_element_type=F32)


def _dot_nt(a, b):
    return lax.dot_general(a, b, (((1,), (1,)), ((), ())), preferred_element_type=F32)


def _params(*sem):
    return pltpu.CompilerParams(dimension_semantics=sem, vmem_limit_bytes=VMEM_LIMIT)


def _ada_kernel(c_ref, w_ref, b_ref, o_ref):
    a = jax.nn.silu(c_ref[...]).astype(BF16)
    o_ref[...] = _dot(a, w_ref[...].astype(BF16)) + b_ref[...]


def _ada(c, w_ada, b_ada):
    depth, d, n = w_ada.shape
    bsz = c.shape[0]
    tn = 1024
    return pl.pallas_call(
        _ada_kernel,
        grid=(depth, n // tn),
        in_specs=[
            pl.BlockSpec((bsz, d), lambda l, j: (0, 0)),
            pl.BlockSpec((None, d, tn), lambda l, j: (l, 0, j)),
            pl.BlockSpec((None, 1, tn), lambda l, j: (l, 0, j)),
        ],
        out_specs=pl.BlockSpec((None, bsz, tn), lambda l, j: (l, 0, j)),
        out_shape=jax.ShapeDtypeStruct((depth, bsz, n), F32),
        compiler_params=_params("parallel", "parallel"),
        name="ada_mod",
    )(c, w_ada, b_ada.reshape(depth, 1, n))


def _prenorm_kernel(x_ref, g_ref, mod_ref, o_ref):
    x = x_ref[...]
    y = x * lax.rsqrt(jnp.mean(x * x, axis=-1, keepdims=True) + EPS) * g_ref[...]
    m = mod_ref[...]
    o_ref[...] = (y * (1.0 + m[1:2]) + m[0:1]).astype(o_ref.dtype)


def _prenorm(x2, g, mod3, seq):
    m, d = x2.shape
    tm = 512
    per_seq = seq // tm
    return pl.pallas_call(
        _prenorm_kernel,
        grid=(m // tm,),
        in_specs=[
            pl.BlockSpec((tm, d), lambda i: (i, 0)),
            pl.BlockSpec((1, d), lambda i: (0, 0)),
            pl.BlockSpec((None, 3, d), lambda i: (i // per_seq, 0, 0)),
        ],
        out_specs=pl.BlockSpec((tm, d), lambda i: (i, 0)),
        out_shape=jax.ShapeDtypeStruct((m, d), BF16),
        compiler_params=_params("parallel"),
        name="prenorm",
    )(x2, g.reshape(1, d), mod3)


def _proj_kernel(a_ref, w_ref, o_ref):
    o_ref[...] = _dot(a_ref[...], w_ref[...]).astype(o_ref.dtype)


def _proj(a, w, out_dtype, tm=1024, tn=1280):
    m, k = a.shape
    n = w.shape[1]
    tm = min(tm, m)
    return pl.pallas_call(
        _proj_kernel,
        grid=(m // tm, n // tn),
        in_specs=[
            pl.BlockSpec((tm, k), lambda i, j: (i, 0)),
            pl.BlockSpec((k, tn), lambda i, j: (0, j)),
        ],
        out_specs=pl.BlockSpec((tm, tn), lambda i, j: (i, j)),
        out_shape=jax.ShapeDtypeStruct((m, n), out_dtype),
        compiler_params=_params("parallel", "arbitrary"),
        name="in_proj",
    )(a, w)


def _proj_sgu_kernel(h_ref, wu_ref, wv_ref, wg_ref, lng_ref, lnb_ref, ws_ref, bst_ref, o_ref,
                     v_buf, p_buf):
    j = pl.program_id(1)
    n_j, tm, tn = v_buf.shape
    cg = D_MODEL // A_GROUPS
    h = h_ref[...]
    v_buf[j] = jax.nn.gelu(_dot(h, wv_ref[...]))
    p_buf[j] = jax.nn.gelu(_dot(h, wu_ref[...])) * jax.nn.silu(_dot(h, wg_ref[...]))

    @pl.when(j == n_j - 1)
    def _():
        total = jnp.sum(v_buf[0], axis=-1, keepdims=True)
        for jj in range(1, n_j):
            total = total + jnp.sum(v_buf[jj], axis=-1, keepdims=True)
        mu = total * (1.0 / D_MODEL)
        sq = jnp.zeros_like(mu)
        for jj in range(n_j):
            dv = v_buf[jj] - mu
            sq = sq + jnp.sum(dv * dv, axis=-1, keepdims=True)
        rstd = lax.rsqrt(sq * (1.0 / D_MODEL) + EPS)
        row = lax.broadcasted_iota(jnp.int32, (A_CHUNK, A_CHUNK), 0)
        col = lax.broadcasted_iota(jnp.int32, (A_CHUNK, A_CHUNK), 1)
        tril = row >= col
        bst = bst_ref[...]
        for g in range(A_GROUPS):
            ws = jnp.where(tril, ws_ref[g], 0.0).astype(BF16)
            jj, off = divmod(g * cg, tn)
            cs = slice(g * cg, (g + 1) * cg)
            for c in range(tm // A_CHUNK):
                rs = slice(c * A_CHUNK, (c + 1) * A_CHUNK)
                vn = ((v_buf[jj, rs, off:off + cg] - mu[rs]) * rstd[rs] * lng_ref[:, cs]
                      + lnb_ref[:, cs]).astype(BF16)
                mixed = _dot(ws, vn) + bst[:, g:g + 1]
                o_ref[rs, cs] = (p_buf[jj, rs, off:off + cg] * mixed).astype(o_ref.dtype)


def _proj_sgu(h, wa, ln_g, ln_b, w_s, b_s):
    m, d = h.shape
    tm, tn = 512, 512
    per = d // tn
    return pl.pallas_call(
        _proj_sgu_kernel,
        grid=(m // tm, per),
        in_specs=[
            pl.BlockSpec((tm, d), lambda i, j: (i, 0)),
            pl.BlockSpec((d, tn), lambda i, j: (0, j)),
            pl.BlockSpec((d, tn), lambda i, j: (0, per + j)),
            pl.BlockSpec((d, tn), lambda i, j: (0, 2 * per + j)),
            pl.BlockSpec((1, d), lambda i, j: (0, 0)),
            pl.BlockSpec((1, d), lambda i, j: (0, 0)),
            pl.BlockSpec((A_GROUPS, A_CHUNK, A_CHUNK), lambda i, j: (0, 0, 0)),
            pl.BlockSpec((A_CHUNK, A_GROUPS), lambda i, j: (0, 0)),
        ],
        out_specs=pl.BlockSpec((tm, d), lambda i, j: (i, 0)),
        out_shape=jax.ShapeDtypeStruct((m, d), BF16),
        scratch_shapes=[pltpu.VMEM((per, tm, tn), F32), pltpu.VMEM((per, tm, tn), F32)],
        compiler_params=_params("parallel", "arbitrary"),
        name="proj_sgu",
    )(h, wa, wa, wa, ln_g.reshape(1, d), ln_b.reshape(1, d), w_s, b_s.T)


def _pool_kernel(x_ref, halo_ref, gate_ref, wg_ref, ls_ref, o_ref, ext_ref, *, tiles_per_seq):
    tm = x_ref.shape[0]
    i = pl.program_id(0)
    pos0 = (i % tiles_per_seq) * tm
    halo = halo_ref[...].astype(F32)
    ext_ref[0:POOL_HALO, :] = jnp.where(pos0 == 0, jnp.zeros_like(halo), halo)
    ext_ref[POOL_HALO:POOL_HALO + tm, :] = x_ref[...].astype(F32)
    tpos = pos0 + lax.broadcasted_iota(jnp.int32, (tm, 1), 0)
    for gi, w in enumerate(POOL_WINDOWS):
        cs = slice(gi * C_GROUP, (gi + 1) * C_GROUP)
        x = ext_ref[POOL_HALO:POOL_HALO + tm, cs]
        s = x
        for back in range(1, w):
            s = s + ext_ref[POOL_HALO - back:POOL_HALO - back + tm, cs]
        cnt = jnp.minimum(tpos + 1, w).astype(F32)
        y = (s / cnt - x).astype(BF16)
        yg = (_dot(y, wg_ref[gi].astype(BF16)) * ls_ref[:, cs]
              * jax.nn.silu(gate_ref[:, cs].astype(F32)))
        o_ref[:, cs] = yg.astype(o_ref.dtype)


def _pool(zf, w_grp, ls, seq):
    m = zf.shape[0]
    tm = 256
    d = D_MODEL
    halo_blocks = tm // POOL_HALO
    return pl.pallas_call(
        functools.partial(_pool_kernel, tiles_per_seq=seq // tm),
        grid=(m // tm,),
        in_specs=[
            pl.BlockSpec((tm, d), lambda i: (i, Z_CX // d)),
            pl.BlockSpec((POOL_HALO, d), lambda i: (jnp.maximum(i * halo_blocks - 1, 0), Z_CX // d)),
            pl.BlockSpec((tm, d), lambda i: (i, Z_CG // d)),
            pl.BlockSpec((len(POOL_WINDOWS), C_GROUP, C_GROUP), lambda i: (0, 0, 0)),
            pl.BlockSpec((1, d), lambda i: (0, 0)),
        ],
        out_specs=pl.BlockSpec((tm, d), lambda i: (i, 0)),
        out_shape=jax.ShapeDtypeStruct((m, d), BF16),
        scratch_shapes=[pltpu.VMEM((POOL_HALO + tm, d), F32)],
        compiler_params=_params("parallel"),
        name="pool",
    )(zf, zf, zf, w_grp, ls.reshape(1, d))


def _compress_kernel(x_ref, w1_ref, w2_ref, pos_ref, o_ref, xf_ref):
    n_chunks = x_ref.shape[0] // CMP_STRIDE
    xf_ref[...] = x_ref[...].astype(F32)
    pos = pos_ref[...]
    first = jnp.zeros((n_chunks, HEAD_DIM), F32)
    second = jnp.zeros((n_chunks, HEAD_DIM), F32)
    for l in range(CMP_STRIDE):
        xl = xf_ref[pl.ds(l, n_chunks, stride=CMP_STRIDE), :]
        first += _dot((xl + pos[l:l + 1]).astype(BF16), w1_ref[l].astype(BF16))
        second += _dot((xl + pos[CMP_STRIDE + l:CMP_STRIDE + l + 1]).astype(BF16),
                       w1_ref[CMP_STRIDE + l].astype(BF16))
    hdn = jax.nn.gelu(first + pltpu.roll(second, n_chunks - 1, axis=0))
    o_ref[...] = _dot(hdn.astype(BF16), w2_ref[...].astype(BF16)).astype(o_ref.dtype)


def _compress(z, w1, w2, pos, bsz, seq):
    n_chunks = seq // CMP_STRIDE
    kv0 = Z_KV // HEAD_DIM
    return pl.pallas_call(
        _compress_kernel,
        grid=(bsz, 2, KV_GROUPS),
        in_specs=[
            pl.BlockSpec((seq, HEAD_DIM), lambda b, t, g: (b, kv0 + t * KV_GROUPS + g)),
            pl.BlockSpec((None, CMP_BLOCK, HEAD_DIM, HEAD_DIM), lambda b, t, g: (t, 0, 0, 0)),
            pl.BlockSpec((None, HEAD_DIM, HEAD_DIM), lambda b, t, g: (t, 0, 0)),
            pl.BlockSpec((None, CMP_BLOCK, HEAD_DIM), lambda b, t, g: (t, 0, 0)),
        ],
        out_specs=pl.BlockSpec((None, None, None, n_chunks, HEAD_DIM), lambda b, t, g: (b, t, g, 0, 0)),
        out_shape=jax.ShapeDtypeStruct((bsz, 2, KV_GROUPS, n_chunks, HEAD_DIM), BF16),
        scratch_shapes=[pltpu.VMEM((seq, HEAD_DIM), F32)],
        compiler_params=_params("parallel", "parallel", "parallel"),
        name="compress",
    )(z, w1, w2, pos)


def _bucket_of_distance(n):
    n = np.asarray(n, np.int64)
    max_exact = REL_BUCKETS // 2
    nf = np.maximum(n, 1).astype(np.float64)
    large = max_exact + (np.log(nf / max_exact) / math.log(REL_MAX_DIST / max_exact)
                         * (REL_BUCKETS - max_exact)).astype(np.int64)
    large = np.minimum(large, REL_BUCKETS - 1)
    return np.where(n < max_exact, n, large).astype(np.int32)


def _bias_kernel(tab_ref, idx_tile_ref, idx_cmp_ref, tile_ref, cmp_ref):
    h = pl.program_id(0) * HEADS_PER_GROUP + pl.program_id(1)
    far = tab_ref[REL_BUCKETS - 1, h]
    for idx_ref, out_ref, shift in ((idx_tile_ref, tile_ref, far), (idx_cmp_ref, cmp_ref, 0.0)):
        idx = idx_ref[...]
        acc = jnp.full(idx.shape, NEG_INF, F32)
        for b in range(REL_BUCKETS):
            acc = jnp.where(idx == b, (tab_ref[b, h] - shift) * LOG2_E, acc)
        out_ref[...] = acc


N_BIAS_TILES = 7
MASKED_TILE = N_BIAS_TILES - 1


def _bias_tables(rel_bias, seq):
    t = ATT_TILE
    kt = KEY_TILE
    n_q = seq // t
    r = np.arange(t)[None, :]
    masked = REL_BUCKETS

    def tile(offset, windowed):
        d = offset + r - np.arange(kt)[:, None]
        ok = (d >= 0) & ((d < WINDOW) | (not windowed))
        return np.where(ok, _bucket_of_distance(np.maximum(d, 0)), masked)

    idx_tile = np.stack([tile(0, False), tile(t, False), tile(2 * t, False), tile(3 * t, False),
                         tile(4 * t, True), tile(5 * t, True),
                         np.full((kt, t), masked)]).astype(np.int32)
    assert idx_tile.shape[0] == N_BIAS_TILES and (idx_tile[3] == REL_BUCKETS - 1).all()
    assert (tile(3 * t, True) == idx_tile[3]).all()
    cmp_end = np.arange(t)[:, None] * CMP_STRIDE + CMP_BLOCK - 1
    idx_cmp = np.stack([_bucket_of_distance(np.maximum(i * t + r - cmp_end, 0))
                        for i in range(n_q)]).astype(np.int32)
    return pl.pallas_call(
        _bias_kernel,
        grid=(KV_GROUPS, HEADS_PER_GROUP),
        in_specs=[
            pl.BlockSpec(memory_space=pltpu.SMEM),
            pl.BlockSpec((N_BIAS_TILES, kt, t), lambda g, j: (0, 0, 0)),
            pl.BlockSpec((n_q, t, t), lambda g, j: (0, 0, 0)),
        ],
        out_specs=[
            pl.BlockSpec((None, N_BIAS_TILES, kt, t), lambda g, j: (g, 0, 0, j)),
            pl.BlockSpec((None, n_q, t, t), lambda g, j: (g, 0, 0, j)),
        ],
        out_shape=[
            jax.ShapeDtypeStruct((KV_GROUPS, N_BIAS_TILES, kt, HEADS_PER_GROUP * t), F32),
            jax.ShapeDtypeStruct((KV_GROUPS, n_q, t, HEADS_PER_GROUP * t), F32),
        ],
        compiler_params=_params("parallel", "parallel"),
        name="bias_tables",
    )(rel_bias, jnp.asarray(idx_tile), jnp.asarray(idx_cmp))


def _attn_kernel(q_ref, ks_ref, vs_ref, kw_ref, vw_ref, kc_ref, vc_ref, gsel_ref, bgate_ref,
                 btile_ref, bcmp_ref, ov_ref, expand_ref, eye_ref, o_ref,
                 vst_ref, vwt_ref, kaug_ref, s_ref, w_ref, osel_ref):
    t = ATT_TILE
    kt = KEY_TILE
    hg = HEADS_PER_GROUP
    cols = hg * t
    n_kt = s_ref.shape[0]
    n_q = n_kt * kt // t
    i = pl.program_id(1)
    last_tile = lax.shift_right_logical(i, 1)
    eye = eye_ref[...]

    @pl.when(i == 0)
    def _():
        kaug_ref[:, 0:HEAD_DIM] = ks_ref[...]
        kaug_ref[:, HEAD_DIM:] = expand_ref[...]
        for jt in range(n_kt):
            ks = slice(jt * kt, (jt + 1) * kt)
            vst_ref[jt] = _dot_nt(eye, vs_ref[ks, :]).astype(BF16)
            vwt_ref[jt] = _dot_nt(eye, vw_ref[ks, :]).astype(BF16)

    qb = q_ref[...]
    qs = jnp.concatenate([qb[:, j * HEAD_DIM:(j + 1) * HEAD_DIM] for j in range(hg)], axis=0)

    n_idx = lax.broadcasted_iota(jnp.int32, (t, cols), 0)
    tq = i * t + (lax.broadcasted_iota(jnp.int32, (t, cols), 1) & (t - 1))
    valid = tq >= n_idx * CMP_STRIDE + (CMP_BLOCK - 1)
    s = jnp.where(valid, _dot_nt(kc_ref[...], qs) + bcmp_ref[...], NEG_INF)
    e = jnp.where(valid, jnp.exp2(s - jnp.max(s, axis=0, keepdims=True)), 0.0)
    l = jnp.sum(e, axis=0, keepdims=True)
    p_c = e * (1.0 / jnp.where(l > 0.0, l, 1.0))
    vct = _dot_nt(eye, vc_ref[...]).astype(BF16)
    ocmp_t = _dot(vct, p_c.astype(BF16))

    psum = p_c[:, 0:t]
    for j in range(1, hg):
        psum = psum + p_c[:, j * t:(j + 1) * t]
    p_hi = psum.astype(BF16)
    p_lo = (psum - p_hi.astype(F32)).astype(BF16)
    ov = ov_ref[...]
    n_sel = n_kt * kt // SEL_BLOCK
    imp = (_dot(ov, p_hi) + _dot(ov, p_lo))[0:n_sel]
    blk = lax.broadcasted_iota(jnp.int32, (n_sel, t), 0)
    tqb = i * t + lax.broadcasted_iota(jnp.int32, (n_sel, t), 1)
    cur = lax.shift_right_logical(tqb, int(math.log2(SEL_BLOCK)))
    forced = (blk == 0) | (blk == cur) | (blk == cur - 1)
    future = blk * SEL_BLOCK > tqb
    imp = jnp.where(future, -1.0, imp + jnp.where(forced, FORCE_BONUS, 0.0))
    rank = jnp.zeros((n_sel, t), F32)
    for sp in range(n_sel):
        other = imp[sp:sp + 1, :]
        beats = (other > imp) | ((other == imp) & (blk > sp))
        rank = rank + jnp.where(beats, 1.0, 0.0)
    unsel = jnp.where(rank < float(min(N_SELECT, n_sel)), 0.0, 1.0)
    unsel_pad = jnp.concatenate([unsel, jnp.zeros((t - n_sel, t), F32)], axis=0).astype(BF16)
    unsel_qb = _dot_nt(eye, unsel_pad).astype(BF16)
    q_aug = jnp.concatenate([qs, jnp.concatenate([unsel_qb] * hg, axis=0)], axis=1)

    def bias_index(delta):
        return jnp.where(delta < 0, MASKED_TILE, delta)

    def attend(q_op, k_tile, bias_tile, vt_tile, n_sub, sc_ref):
        mx = jnp.full((kt, cols), NEG_INF, F32)
        for js in range(n_sub):
            st = _dot_nt(k_tile(js), q_op)
            bias = bias_tile(js)
            if bias is not None:
                st = st + bias
            sc_ref[js] = st
            mx = jnp.maximum(mx, st)
        m = jnp.max(mx, axis=0, keepdims=True)
        lsum = jnp.zeros((kt, cols), F32)
        acc = jnp.zeros((HEAD_DIM, cols), F32)
        for js in range(n_sub):
            p = jnp.exp2(sc_ref[js] - m)
            lsum = lsum + p
            acc = acc + _dot(vt_tile(js), p.astype(BF16))
        return acc * (1.0 / jnp.sum(lsum, axis=0, keepdims=True))

    win_tiles = WINDOW // kt + 1
    first = jnp.maximum(last_tile - (win_tiles - 1), 0)

    def win_k(js):
        return kw_ref[pl.ds(pl.multiple_of((first + js) * kt, kt), kt), :]

    owin_t = attend(qs, win_k, lambda js: btile_ref[bias_index(i - 2 * (first + js))],
                    lambda js: vwt_ref[first + js], win_tiles, w_ref)

    tiles_per_variant = 1
    q_per_variant = tiles_per_variant * kt // t
    far_offset = 3
    for variant in range(n_q // q_per_variant):
        n_sub = tiles_per_variant * (variant + 1)
        n_far = max((q_per_variant * variant - far_offset) // 2 + 1, 0)

        def sel_bias(js, n_far=n_far):
            if js < n_far:
                return None
            return btile_ref[bias_index(jnp.minimum(i - 2 * js, far_offset))]

        @pl.when(i // q_per_variant == variant)
        def _(n_sub=n_sub, sel_bias=sel_bias):
            osel_ref[...] = attend(q_aug, lambda js: kaug_ref[js * kt:(js + 1) * kt, :], sel_bias,
                                   lambda js: vst_ref[js], n_sub, s_ref)

    osel_t = osel_ref[...]

    gt = jnp.transpose(jax.nn.sigmoid(gsel_ref[...].astype(F32)))
    for j in range(hg):
        cs = slice(j * t, (j + 1) * t)
        ot = (gt[j:j + 1] * ocmp_t[:, cs] + gt[hg + j:hg + j + 1] * osel_t[:, cs]
              + gt[2 * hg + j:2 * hg + j + 1] * owin_t[:, cs])
        hs = slice(j * HEAD_DIM, (j + 1) * HEAD_DIM)
        o_ref[:, hs] = (jnp.transpose(ot)
                        * jax.nn.silu(bgate_ref[:, hs].astype(F32))).astype(o_ref.dtype)


def _attention_constants(seq):
    t = ATT_TILE
    n_cmp_pad = seq // CMP_STRIDE
    assert n_cmp_pad == t, "one tile of compressed keys per sequence"
    n_cmp = n_cmp_pad - CMP_BLOCK // CMP_STRIDE + 1
    n_sel = seq // SEL_BLOCK
    cst = np.arange(n_cmp_pad) * CMP_STRIDE
    sst = np.arange(n_sel) * SEL_BLOCK
    overlap = ((cst[:, None] < sst[None] + SEL_BLOCK) & (cst[:, None] + CMP_BLOCK > sst[None]))
    overlap[n_cmp:] = False
    ov = np.zeros((t, n_cmp_pad), np.float32)
    ov[:n_sel] = overlap.T
    keys = np.arange(seq)
    expand = np.zeros((seq, t), np.float32)
    expand[keys, keys // SEL_BLOCK] = NEG_INF
    eye = np.eye(t, dtype=np.float32)
    return (jnp.asarray(ov, BF16), jnp.asarray(expand, BF16), jnp.asarray(eye, BF16))


def _attention(z, kvc, bias_tile, bias_cmp, consts, bsz, seq):
    t = ATT_TILE
    kt = KEY_TILE
    n_q = seq // t
    n_kt = seq // kt
    gw = HEADS_PER_GROUP * HEAD_DIM
    kv0 = Z_KV // HEAD_DIM
    ov, expand, eye = consts

    def kv_spec(kind):
        return pl.BlockSpec((seq, HEAD_DIM),
                            lambda bg, i: (bg // KV_GROUPS, kv0 + kind * KV_GROUPS + bg % KV_GROUPS))

    def cmp_spec(kind):
        return pl.BlockSpec((None, None, None, t, HEAD_DIM),
                            lambda bg, i: (bg // KV_GROUPS, kind, bg % KV_GROUPS, 0, 0))

    def tok(bg, i):
        return (bg // KV_GROUPS) * n_q + i

    return pl.pallas_call(
        _attn_kernel,
        grid=(bsz * KV_GROUPS, n_q),
        in_specs=[
            pl.BlockSpec((t, gw), lambda bg, i: (tok(bg, i), Z_Q // gw + bg % KV_GROUPS)),
            kv_spec(KV_KINDS.index("k_s")), kv_spec(KV_KINDS.index("v_s")),
            kv_spec(KV_KINDS.index("k_w")), kv_spec(KV_KINDS.index("v_w")),
            cmp_spec(0), cmp_spec(1),
            pl.BlockSpec((t, V7X_LANES), lambda bg, i: (tok(bg, i), Z_GSEL // V7X_LANES + bg % KV_GROUPS)),
            pl.BlockSpec((t, gw), lambda bg, i: (tok(bg, i), Z_BG // gw + bg % KV_GROUPS)),
            pl.BlockSpec((None, N_BIAS_TILES, kt, gw), lambda bg, i: (bg % KV_GROUPS, 0, 0, 0)),
            pl.BlockSpec((None, None, t, gw), lambda bg, i: (bg % KV_GROUPS, i, 0, 0)),
            pl.BlockSpec((t, t), lambda bg, i: (0, 0)),
            pl.BlockSpec((seq, t), lambda bg, i: (0, 0)),
            pl.BlockSpec((t, t), lambda bg, i: (0, 0)),
        ],
        out_specs=pl.BlockSpec((t, gw), lambda bg, i: (tok(bg, i), bg % KV_GROUPS)),
        out_shape=jax.ShapeDtypeStruct((bsz * seq, D_MODEL), BF16),
        scratch_shapes=[pltpu.VMEM((n_kt, HEAD_DIM, kt), BF16),
                        pltpu.VMEM((n_kt, HEAD_DIM, kt), BF16),
                        pltpu.VMEM((seq, HEAD_DIM + t), BF16),
                        pltpu.VMEM((n_kt, kt, gw), F32),
                        pltpu.VMEM((WINDOW // kt + 1, kt, gw), F32),
                        pltpu.VMEM((HEAD_DIM, gw), F32)],
        compiler_params=_params("parallel", "arbitrary"),
        name="nsa_attention",
    )(z, z, z, z, z, kvc, kvc, z, z, bias_tile, bias_cmp, ov, expand, eye)


def _merge_kernel(ya_ref, yb_ref, yc_ref, ma_ref, mb_ref, mc_ref, w_ref, o_ref):
    y = jax.nn.sigmoid(ma_ref[...].astype(F32)) * _dot(ya_ref[...], w_ref[0])
    y += jax.nn.sigmoid(mb_ref[...].astype(F32)) * _dot(yb_ref[...], w_ref[1])
    y += jax.nn.sigmoid(mc_ref[...].astype(F32)) * _dot(yc_ref[...], w_ref[2])
    o_ref[...] = y.astype(o_ref.dtype)


def _merge(ya, yb, yc, zf, w_branch):
    m, d = ya.shape
    tm, tn = min(1024, m), 256
    m0 = Z_MERGE // tn
    per = d // tn
    y_spec = pl.BlockSpec((tm, d), lambda i, j: (i, 0))
    return pl.pallas_call(
        _merge_kernel,
        grid=(m // tm, d // tn),
        in_specs=[
            y_spec, y_spec, y_spec,
            pl.BlockSpec((tm, tn), lambda i, j: (i, m0 + j)),
            pl.BlockSpec((tm, tn), lambda i, j: (i, m0 + per + j)),
            pl.BlockSpec((tm, tn), lambda i, j: (i, m0 + 2 * per + j)),
            pl.BlockSpec((3, d, tn), lambda i, j: (0, 0, j)),
        ],
        out_specs=pl.BlockSpec((tm, tn), lambda i, j: (i, j)),
        out_shape=jax.ShapeDtypeStruct((m, d), BF16),
        compiler_params=_params("parallel", "arbitrary"),
        name="merge",
    )(ya, yb, yc, zf, zf, zf, w_branch)


def _out_kernel(y_ref, w_ref, x_ref, mod_ref, g_ref, *rest, last):
    x = x_ref[...] + mod_ref[2:3, :] * _dot(y_ref[...], w_ref[...])
    normed = x * lax.rsqrt(jnp.mean(x * x, axis=-1, keepdims=True) + EPS) * g_ref[...]
    if last:
        (o_ref,) = rest
        o_ref[...] = normed
    else:
        next_mod_ref, x_out_ref, h_ref = rest
        nm = next_mod_ref[...]
        x_out_ref[...] = x
        h_ref[...] = (normed * (1.0 + nm[1:2]) + nm[0:1]).astype(h_ref.dtype)


def _out_proj(y, w_out, x2, mod3, g, next_mod3, seq):
    m, d = y.shape
    tm = 512
    per_seq = seq // tm
    last = next_mod3 is None
    row = pl.BlockSpec((tm, d), lambda i: (i, 0))
    mod_spec = pl.BlockSpec((None, 3, d), lambda i: (i // per_seq, 0, 0))
    in_specs = [row, pl.BlockSpec((d, d), lambda i: (0, 0)), row, mod_spec,
                pl.BlockSpec((1, d), lambda i: (0, 0))]
    args = [y, w_out, x2, mod3, g.reshape(1, d)]
    if last:
        out_specs = row
        out_shape = jax.ShapeDtypeStruct((m, d), F32)
    else:
        in_specs.append(mod_spec)
        args.append(next_mod3)
        out_specs = [row, row]
        out_shape = [jax.ShapeDtypeStruct((m, d), F32), jax.ShapeDtypeStruct((m, d), BF16)]
    return pl.pallas_call(
        functools.partial(_out_kernel, last=last),
        grid=(m // tm,),
        in_specs=in_specs,
        out_specs=out_specs,
        out_shape=out_shape,
        compiler_params=_params("parallel"),
        name="out_proj",
    )(*args)


def _split_w_in(w):
    d = D_MODEL
    kvw = KV_GROUPS * HEAD_DIM
    sizes = (d, d, d, d, 6 * kvw, 3 * N_HEADS, d, d, d, 3 * d)
    offs = np.concatenate([[0], np.cumsum(sizes)])
    a_u, a_v, a_g, q, kv, gsel, b_g, c_x, c_g, merge = [w[:, offs[k]:offs[k + 1]] for k in range(10)]
    gs = gsel.reshape(d, 3, KV_GROUPS, HEADS_PER_GROUP).transpose(0, 2, 1, 3)
    gs = gs.reshape(d, KV_GROUPS, 3 * HEADS_PER_GROUP)
    gs = jnp.pad(gs, ((0, 0), (0, 0), (0, V7X_LANES - 3 * HEADS_PER_GROUP))).reshape(d, KV_GROUPS * V7X_LANES)
    q_scale = HEAD_DIM ** -0.5 * LOG2_E
    wz = jnp.concatenate([q * q_scale, c_x, c_g, b_g, merge, kv, gs], axis=1)
    wa = jnp.concatenate([a_u, a_v, a_g], axis=1)
    assert wz.shape[1] == Z_COLS
    return wz.astype(BF16), wa.astype(BF16)


def kernel(x, c, rel_bias, norm_g, w_ada, b_ada, w_in, a_ln_g, a_ln_b, a_w_s, a_b_s, b_w_cmp1,
           b_w_cmp2, b_pos_cmp, c_w_grp, c_scale, w_branch, w_out, final_g):
    bsz, seq, d = x.shape
    depth = w_in.shape[0]
    assert d == D_MODEL and seq % 512 == 0
    x2 = x.reshape(bsz * seq, d)
    mod = _ada(c, w_ada, b_ada)
    bias_tile, bias_cmp = _bias_tables(rel_bias, seq)
    consts = _attention_constants(seq)
    mod3 = [mod[l].reshape(bsz, 3, d) for l in range(depth)]
    h = _prenorm(x2, norm_g[0], mod3[0], seq)
    for l in range(depth):
        wz, wa = _split_w_in(w_in[l])
        z = _proj(h, wz, BF16)
        ya = _proj_sgu(h, wa, a_ln_g[l], a_ln_b[l], a_w_s[l], a_b_s[l])
        yc = _pool(z, c_w_grp[l], c_scale[l], seq)
        kvc = _compress(z, b_w_cmp1[l], b_w_cmp2[l], b_pos_cmp[l], bsz, seq)
        yb = _attention(z, kvc, bias_tile, bias_cmp, consts, bsz, seq)
        y = _merge(ya, yb, yc, z, w_branch[l].astype(BF16))
        if l + 1 < depth:
            x2, h = _out_proj(y, w_out[l].astype(BF16), x2, mod3[l], norm_g[l + 1], mod3[l + 1], seq)
        else:
            out = _out_proj(y, w_out[l].astype(BF16), x2, mod3[l], final_g, None, seq)
    return out.reshape(bsz, seq, d)
```

```python
import functools
import math

import numpy as np
import jax
import jax.numpy as jnp
from jax import lax
from jax.experimental import pallas as pl
from jax.experimental.pallas import tpu as pltpu

D_MODEL = 2048
A_GROUPS = 8
A_CHUNK = 128
HEAD_DIM = 128
N_HEADS = 16
KV_GROUPS = 4
HEADS_PER_GROUP = N_HEADS // KV_GROUPS
CMP_BLOCK = 32
CMP_STRIDE = 16
SEL_BLOCK = 64
N_SELECT = 16
WINDOW = 512
FORCE_BONUS = 1e4
POOL_WINDOWS = (2, 4, 8, 16)
C_GROUP = D_MODEL // len(POOL_WINDOWS)
REL_BUCKETS = 32
REL_MAX_DIST = 128
EPS = 1e-6
NEG_INF = -1e30

V7X_LANES = 128
V7X_VMEM_BYTES = 64 * 1024 * 1024
VMEM_LIMIT = 48 * 1024 * 1024

ATT_TILE = 128
KEY_TILE = 256
LOG2_E = math.log2(math.e)
POOL_HALO = 16
BF16 = jnp.bfloat16
F32 = jnp.float32

Z_Q = 0
Z_BG = 2048
Z_CX, Z_CG = 4096, 6144
Z_MERGE = 8192
Z_KV = 14336
Z_GSEL = 17408
Z_COLS = 17920
W_A = Z_COLS
W_COLS = W_A + 3 * D_MODEL
KV_KINDS = ("k_c", "v_c", "k_s", "v_s", "k_w", "v_w")


def _dot(a, b):
    return jnp.dot(a, b, preferred_element_type=F32)


def _dot_nt(a, b):
    return lax.dot_general(a, b, (((1,), (1,)), ((), ())), preferred_element_type=F32)


def _params(*sem):
    return pltpu.CompilerParams(dimension_semantics=sem, vmem_limit_bytes=VMEM_LIMIT)


def _ada_kernel(c_ref, w_ref, b_ref, o_ref):
    a = jax.nn.silu(c_ref[...]).astype(BF16)
    o_ref[...] = _dot(a, w_ref[...].astype(BF16)) + b_ref[...]


def _ada(c, w_ada, b_ada):
    depth, d, n = w_ada.shape
    bsz = c.shape[0]
    tn = 1024
    return pl.pallas_call(
        _ada_kernel,
        grid=(depth, n // tn),
        in_specs=[
            pl.BlockSpec((bsz, d), lambda l, j: (0, 0)),
            pl.BlockSpec((None, d, tn), lambda l, j: (l, 0, j)),
            pl.BlockSpec((None, 1, tn), lambda l, j: (l, 0, j)),
        ],
        out_specs=pl.BlockSpec((None, bsz, tn), lambda l, j: (l, 0, j)),
        out_shape=jax.ShapeDtypeStruct((depth, bsz, n), F32),
        compiler_params=_params("parallel", "parallel"),
        name="ada_mod",
    )(c, w_ada, b_ada.reshape(depth, 1, n))


def _prenorm_kernel(x_ref, g_ref, mod_ref, o_ref):
    x = x_ref[...]
    y = x * lax.rsqrt(jnp.mean(x * x, axis=-1, keepdims=True) + EPS) * g_ref[...]
    m = mod_ref[...]
    o_ref[...] = (y * (1.0 + m[1:2]) + m[0:1]).astype(o_ref.dtype)


def _prenorm(x2, g, mod3, seq):
    m, d = x2.shape
    tm = 512
    per_seq = seq // tm
    return pl.pallas_call(
        _prenorm_kernel,
        grid=(m // tm,),
        in_specs=[
            pl.BlockSpec((tm, d), lambda i: (i, 0)),
            pl.BlockSpec((1, d), lambda i: (0, 0)),
            pl.BlockSpec((None, 3, d), lambda i: (i // per_seq, 0, 0)),
        ],
        out_specs=pl.BlockSpec((tm, d), lambda i: (i, 0)),
        out_shape=jax.ShapeDtypeStruct((m, d), BF16),
        compiler_params=_params("parallel"),
        name="prenorm",
    )(x2, g.reshape(1, d), mod3)


def _proj_kernel(a_ref, w_ref, o_ref):
    o_ref[...] = _dot(a_ref[...], w_ref[...]).astype(o_ref.dtype)


def _proj(a, w_all, layer, n, out_dtype, tm=1024, tn=1280):
    m, k = a.shape
    tm = min(tm, m)
    return pl.pallas_call(
        _proj_kernel,
        grid=(m // tm, n // tn),
        in_specs=[
            pl.BlockSpec((tm, k), lambda i, j: (i, 0)),
            pl.BlockSpec((None, k, tn), lambda i, j: (layer, 0, j)),
        ],
        out_specs=pl.BlockSpec((tm, tn), lambda i, j: (i, j)),
        out_shape=jax.ShapeDtypeStruct((m, n), out_dtype),
        compiler_params=_params("parallel", "arbitrary"),
        name="in_proj",
    )(a, w_all)


def _proj_sgu_kernel(h_ref, wu_ref, wv_ref, wg_ref, lng_ref, lnb_ref, ws_ref, bst_ref, o_ref,
                     v_buf, p_buf):
    j = pl.program_id(1)
    n_j, tm, tn = v_buf.shape
    cg = D_MODEL // A_GROUPS
    h = h_ref[...]
    v_buf[j] = jax.nn.gelu(_dot(h, wv_ref[...]))
    p_buf[j] = jax.nn.gelu(_dot(h, wu_ref[...])) * jax.nn.silu(_dot(h, wg_ref[...]))

    @pl.when(j == n_j - 1)
    def _():
        total = jnp.sum(v_buf[0], axis=-1, keepdims=True)
        for jj in range(1, n_j):
            total = total + jnp.sum(v_buf[jj], axis=-1, keepdims=True)
        mu = total * (1.0 / D_MODEL)
        sq = jnp.zeros_like(mu)
        for jj in range(n_j):
            dv = v_buf[jj] - mu
            sq = sq + jnp.sum(dv * dv, axis=-1, keepdims=True)
        rstd = lax.rsqrt(sq * (1.0 / D_MODEL) + EPS)
        row = lax.broadcasted_iota(jnp.int32, (A_CHUNK, A_CHUNK), 0)
        col = lax.broadcasted_iota(jnp.int32, (A_CHUNK, A_CHUNK), 1)
        tril = row >= col
        bst = bst_ref[...]
        for g in range(A_GROUPS):
            ws = jnp.where(tril, ws_ref[g], 0.0).astype(BF16)
            jj, off = divmod(g * cg, tn)
            cs = slice(g * cg, (g + 1) * cg)
            for c in range(tm // A_CHUNK):
                rs = slice(c * A_CHUNK, (c + 1) * A_CHUNK)
                vn = ((v_buf[jj, rs, off:off + cg] - mu[rs]) * rstd[rs] * lng_ref[:, cs]
                      + lnb_ref[:, cs]).astype(BF16)
                mixed = _dot(ws, vn) + bst[:, g:g + 1]
                o_ref[rs, cs] = (p_buf[jj, rs, off:off + cg] * mixed).astype(o_ref.dtype)


def _proj_sgu(h, w_all, layer, ln_g, ln_b, w_s, b_s):
    m, d = h.shape
    tm, tn = 512, 512
    per = d // tn
    a0 = W_A // tn
    return pl.pallas_call(
        _proj_sgu_kernel,
        grid=(m // tm, per),
        in_specs=[
            pl.BlockSpec((tm, d), lambda i, j: (i, 0)),
            pl.BlockSpec((None, d, tn), lambda i, j: (layer, 0, a0 + j)),
            pl.BlockSpec((None, d, tn), lambda i, j: (layer, 0, a0 + per + j)),
            pl.BlockSpec((None, d, tn), lambda i, j: (layer, 0, a0 + 2 * per + j)),
            pl.BlockSpec((1, d), lambda i, j: (0, 0)),
            pl.BlockSpec((1, d), lambda i, j: (0, 0)),
            pl.BlockSpec((A_GROUPS, A_CHUNK, A_CHUNK), lambda i, j: (0, 0, 0)),
            pl.BlockSpec((A_CHUNK, A_GROUPS), lambda i, j: (0, 0)),
        ],
        out_specs=pl.BlockSpec((tm, d), lambda i, j: (i, 0)),
        out_shape=jax.ShapeDtypeStruct((m, d), BF16),
        scratch_shapes=[pltpu.VMEM((per, tm, tn), F32), pltpu.VMEM((per, tm, tn), F32)],
        compiler_params=_params("parallel", "arbitrary"),
        name="proj_sgu",
    )(h, w_all, w_all, w_all, ln_g.reshape(1, d), ln_b.reshape(1, d), w_s, b_s.T)


def _pool_kernel(x_ref, halo_ref, gate_ref, wg_ref, ls_ref, o_ref, ext_ref, *, tiles_per_seq):
    tm = x_ref.shape[0]
    i = pl.program_id(0)
    pos0 = (i % tiles_per_seq) * tm
    halo = halo_ref[...].astype(F32)
    ext_ref[0:POOL_HALO, :] = jnp.where(pos0 == 0, jnp.zeros_like(halo), halo)
    ext_ref[POOL_HALO:POOL_HALO + tm, :] = x_ref[...].astype(F32)
    tpos = pos0 + lax.broadcasted_iota(jnp.int32, (tm, 1), 0)
    for gi, w in enumerate(POOL_WINDOWS):
        cs = slice(gi * C_GROUP, (gi + 1) * C_GROUP)
        x = ext_ref[POOL_HALO:POOL_HALO + tm, cs]
        s = x
        for back in range(1, w):
            s = s + ext_ref[POOL_HALO - back:POOL_HALO - back + tm, cs]
        cnt = jnp.minimum(tpos + 1, w).astype(F32)
        y = (s / cnt - x).astype(BF16)
        yg = (_dot(y, wg_ref[gi].astype(BF16)) * ls_ref[:, cs]
              * jax.nn.silu(gate_ref[:, cs].astype(F32)))
        o_ref[:, cs] = yg.astype(o_ref.dtype)


def _pool(zf, w_grp, ls, seq):
    m = zf.shape[0]
    tm = 256
    d = D_MODEL
    halo_blocks = tm // POOL_HALO
    return pl.pallas_call(
        functools.partial(_pool_kernel, tiles_per_seq=seq // tm),
        grid=(m // tm,),
        in_specs=[
            pl.BlockSpec((tm, d), lambda i: (i, Z_CX // d)),
            pl.BlockSpec((POOL_HALO, d), lambda i: (jnp.maximum(i * halo_blocks - 1, 0), Z_CX // d)),
            pl.BlockSpec((tm, d), lambda i: (i, Z_CG // d)),
            pl.BlockSpec((len(POOL_WINDOWS), C_GROUP, C_GROUP), lambda i: (0, 0, 0)),
            pl.BlockSpec((1, d), lambda i: (0, 0)),
        ],
        out_specs=pl.BlockSpec((tm, d), lambda i: (i, 0)),
        out_shape=jax.ShapeDtypeStruct((m, d), BF16),
        scratch_shapes=[pltpu.VMEM((POOL_HALO + tm, d), F32)],
        compiler_params=_params("parallel"),
        name="pool",
    )(zf, zf, zf, w_grp, ls.reshape(1, d))


def _compress_kernel(x_ref, w1_ref, w2_ref, pos_ref, o_ref, xf_ref):
    n_chunks = x_ref.shape[0] // CMP_STRIDE
    xf_ref[...] = x_ref[...].astype(F32)
    pos = pos_ref[...]
    first = jnp.zeros((n_chunks, HEAD_DIM), F32)
    second = jnp.zeros((n_chunks, HEAD_DIM), F32)
    for l in range(CMP_STRIDE):
        xl = xf_ref[pl.ds(l, n_chunks, stride=CMP_STRIDE), :]
        first += _dot((xl + pos[l:l + 1]).astype(BF16), w1_ref[l].astype(BF16))
        second += _dot((xl + pos[CMP_STRIDE + l:CMP_STRIDE + l + 1]).astype(BF16),
                       w1_ref[CMP_STRIDE + l].astype(BF16))
    hdn = jax.nn.gelu(first + pltpu.roll(second, n_chunks - 1, axis=0))
    o_ref[...] = _dot(hdn.astype(BF16), w2_ref[...].astype(BF16)).astype(o_ref.dtype)


def _compress(z, w1, w2, pos, bsz, seq):
    n_chunks = seq // CMP_STRIDE
    kv0 = Z_KV // HEAD_DIM
    return pl.pallas_call(
        _compress_kernel,
        grid=(bsz, 2, KV_GROUPS),
        in_specs=[
            pl.BlockSpec((seq, HEAD_DIM), lambda b, t, g: (b, kv0 + t * KV_GROUPS + g)),
            pl.BlockSpec((None, CMP_BLOCK, HEAD_DIM, HEAD_DIM), lambda b, t, g: (t, 0, 0, 0)),
            pl.BlockSpec((None, HEAD_DIM, HEAD_DIM), lambda b, t, g: (t, 0, 0)),
            pl.BlockSpec((None, CMP_BLOCK, HEAD_DIM), lambda b, t, g: (t, 0, 0)),
        ],
        out_specs=pl.BlockSpec((None, None, None, n_chunks, HEAD_DIM), lambda b, t, g: (b, t, g, 0, 0)),
        out_shape=jax.ShapeDtypeStruct((bsz, 2, KV_GROUPS, n_chunks, HEAD_DIM), BF16),
        scratch_shapes=[pltpu.VMEM((seq, HEAD_DIM), F32)],
        compiler_params=_params("parallel", "parallel", "parallel"),
        name="compress",
    )(z, w1, w2, pos)


def _bucket_of_distance(n):
    n = np.asarray(n, np.int64)
    max_exact = REL_BUCKETS // 2
    nf = np.maximum(n, 1).astype(np.float64)
    large = max_exact + (np.log(nf / max_exact) / math.log(REL_MAX_DIST / max_exact)
                         * (REL_BUCKETS - max_exact)).astype(np.int64)
    large = np.minimum(large, REL_BUCKETS - 1)
    return np.where(n < max_exact, n, large).astype(np.int32)


def _bias_kernel(tab_ref, idx_tile_ref, idx_cmp_ref, tile_ref, cmp_ref):
    h = pl.program_id(0) * HEADS_PER_GROUP + pl.program_id(1)
    far = tab_ref[REL_BUCKETS - 1, h]
    for idx_ref, out_ref, shift in ((idx_tile_ref, tile_ref, far), (idx_cmp_ref, cmp_ref, 0.0)):
        idx = idx_ref[...]
        acc = jnp.full(idx.shape, NEG_INF, F32)
        for b in range(REL_BUCKETS):
            acc = jnp.where(idx == b, (tab_ref[b, h] - shift) * LOG2_E, acc)
        out_ref[...] = acc


N_BIAS_TILES = 7
MASKED_TILE = N_BIAS_TILES - 1


def _bias_tables(rel_bias, seq):
    t = ATT_TILE
    kt = KEY_TILE
    n_q = seq // t
    r = np.arange(t)[None, :]
    masked = REL_BUCKETS

    def tile(offset, windowed):
        d = offset + r - np.arange(kt)[:, None]
        ok = (d >= 0) & ((d < WINDOW) | (not windowed))
        return np.where(ok, _bucket_of_distance(np.maximum(d, 0)), masked)

    idx_tile = np.stack([tile(0, False), tile(t, False), tile(2 * t, False), tile(3 * t, False),
                         tile(4 * t, True), tile(5 * t, True),
                         np.full((kt, t), masked)]).astype(np.int32)
    assert idx_tile.shape[0] == N_BIAS_TILES and (idx_tile[3] == REL_BUCKETS - 1).all()
    assert (tile(3 * t, True) == idx_tile[3]).all()
    cmp_end = np.arange(t)[:, None] * CMP_STRIDE + CMP_BLOCK - 1
    idx_cmp = np.stack([_bucket_of_distance(np.maximum(i * t + r - cmp_end, 0))
                        for i in range(n_q)]).astype(np.int32)
    return pl.pallas_call(
        _bias_kernel,
        grid=(KV_GROUPS, HEADS_PER_GROUP),
        in_specs=[
            pl.BlockSpec(memory_space=pltpu.SMEM),
            pl.BlockSpec((N_BIAS_TILES, kt, t), lambda g, j: (0, 0, 0)),
            pl.BlockSpec((n_q, t, t), lambda g, j: (0, 0, 0)),
        ],
        out_specs=[
            pl.BlockSpec((None, N_BIAS_TILES, kt, t), lambda g, j: (g, 0, 0, j)),
            pl.BlockSpec((None, n_q, t, t), lambda g, j: (g, 0, 0, j)),
        ],
        out_shape=[
            jax.ShapeDtypeStruct((KV_GROUPS, N_BIAS_TILES, kt, HEADS_PER_GROUP * t), F32),
            jax.ShapeDtypeStruct((KV_GROUPS, n_q, t, HEADS_PER_GROUP * t), F32),
        ],
        compiler_params=_params("parallel", "parallel"),
        name="bias_tables",
    )(rel_bias, jnp.asarray(idx_tile), jnp.asarray(idx_cmp))


def _attn_kernel(q_ref, ks_ref, vs_ref, kw_ref, vw_ref, kc_ref, vc_ref, gsel_ref, bgate_ref,
                 btile_ref, bcmp_ref, ov_ref, expand_ref, eye_ref, o_ref,
                 vst_ref, vwt_ref, kaug_ref, s_ref, w_ref, osel_ref, owin_ref):
    t = ATT_TILE
    kt = KEY_TILE
    hg = HEADS_PER_GROUP
    cols = hg * t
    n_kt = s_ref.shape[0]
    n_q = n_kt * kt // t
    i = pl.program_id(1)
    last_tile = lax.shift_right_logical(i, 1)
    eye = eye_ref[...]

    @pl.when(i == 0)
    def _():
        kaug_ref[:, 0:HEAD_DIM] = ks_ref[...]
        kaug_ref[:, HEAD_DIM:] = expand_ref[...]
        for jt in range(n_kt):
            ks = slice(jt * kt, (jt + 1) * kt)
            vst_ref[jt] = _dot_nt(eye, vs_ref[ks, :]).astype(BF16)
            vwt_ref[jt] = _dot_nt(eye, vw_ref[ks, :]).astype(BF16)

    qb = q_ref[...]
    qs = jnp.concatenate([qb[:, j * HEAD_DIM:(j + 1) * HEAD_DIM] for j in range(hg)], axis=0)

    def bias_index(delta):
        return jnp.where(delta < 0, MASKED_TILE, delta)

    win_tiles = WINDOW // kt + 1
    win_first = jnp.maximum(last_tile - (win_tiles - 1), 0)

    def win_score_tile(js):
        jt = win_first + js
        k = kw_ref[pl.ds(pl.multiple_of(jt * kt, kt), kt), :]
        st = _dot_nt(k, qs) + btile_ref[bias_index(i - 2 * jt)]
        w_ref[js] = st
        return st

    n_idx = lax.broadcasted_iota(jnp.int32, (t, cols), 0)
    tq = i * t + (lax.broadcasted_iota(jnp.int32, (t, cols), 1) & (t - 1))
    valid = tq >= n_idx * CMP_STRIDE + (CMP_BLOCK - 1)
    s = jnp.where(valid, _dot_nt(kc_ref[...], qs) + bcmp_ref[...], NEG_INF)
    win_mx = win_score_tile(0)
    e = jnp.where(valid, jnp.exp2(s - jnp.max(s, axis=0, keepdims=True)), 0.0)
    l = jnp.sum(e, axis=0, keepdims=True)
    p_c = e * (1.0 / jnp.where(l > 0.0, l, 1.0))
    vct = _dot_nt(eye, vc_ref[...]).astype(BF16)
    ocmp_t = _dot(vct, p_c.astype(BF16))

    psum = p_c[:, 0:t]
    for j in range(1, hg):
        psum = psum + p_c[:, j * t:(j + 1) * t]
    p_hi = psum.astype(BF16)
    p_lo = (psum - p_hi.astype(F32)).astype(BF16)
    ov = ov_ref[...]
    n_sel = n_kt * kt // SEL_BLOCK
    imp = (_dot(ov, p_hi) + _dot(ov, p_lo))[0:n_sel]
    win_mx = jnp.maximum(win_mx, win_score_tile(1))
    blk = lax.broadcasted_iota(jnp.int32, (n_sel, t), 0)
    tqb = i * t + lax.broadcasted_iota(jnp.int32, (n_sel, t), 1)
    cur = lax.shift_right_logical(tqb, int(math.log2(SEL_BLOCK)))
    forced = (blk == 0) | (blk == cur) | (blk == cur - 1)
    future = blk * SEL_BLOCK > tqb
    imp = jnp.where(future, -1.0, imp + jnp.where(forced, FORCE_BONUS, 0.0))
    rank = jnp.zeros((n_sel, t), F32)
    for sp in range(n_sel):
        other = imp[sp:sp + 1, :]
        beats = (other > imp) | ((other == imp) & (blk > sp))
        rank = rank + jnp.where(beats, 1.0, 0.0)
    unsel = jnp.where(rank < float(min(N_SELECT, n_sel)), 0.0, 1.0)
    unsel_pad = jnp.concatenate([unsel, jnp.zeros((t - n_sel, t), F32)], axis=0).astype(BF16)
    unsel_qb = _dot_nt(eye, unsel_pad).astype(BF16)
    for js in range(2, win_tiles):
        win_mx = jnp.maximum(win_mx, win_score_tile(js))
    m_win = jnp.max(win_mx, axis=0, keepdims=True)
    q_aug = jnp.concatenate([qs, jnp.concatenate([unsel_qb] * hg, axis=0)], axis=1)

    def value_tile(carry, m, vt, sc_ref, js):
        lsum, acc = carry
        p = jnp.exp2(sc_ref[js] - m)
        return lsum + p, acc + _dot(vt, p.astype(BF16))

    def normalised(carry):
        lsum, acc = carry
        return acc * (1.0 / jnp.sum(lsum, axis=0, keepdims=True))

    zero_carry = (jnp.zeros((kt, cols), F32), jnp.zeros((HEAD_DIM, cols), F32))

    q_per_variant = kt // t
    far_offset = 3
    for variant in range(n_q // q_per_variant):
        n_sub = variant + 1
        n_far = max((q_per_variant * variant - far_offset) // 2 + 1, 0)
        first = max(variant - (win_tiles - 1), 0)

        def sel_bias(js, n_far=n_far):
            if js < n_far:
                return None
            return btile_ref[bias_index(jnp.minimum(i - 2 * js, far_offset))]

        @pl.when(last_tile == variant)
        def _(n_sub=n_sub, sel_bias=sel_bias, first=first):
            mx = jnp.full((kt, cols), NEG_INF, F32)
            for js in range(n_sub):
                st = _dot_nt(kaug_ref[js * kt:(js + 1) * kt, :], q_aug)
                bias = sel_bias(js)
                if bias is not None:
                    st = st + bias
                s_ref[js] = st
                mx = jnp.maximum(mx, st)
            win = zero_carry
            for js in range(win_tiles):
                win = value_tile(win, m_win, vwt_ref[first + js], w_ref, js)
            owin_ref[...] = normalised(win)
            m_sel = jnp.max(mx, axis=0, keepdims=True)
            sel = zero_carry
            for js in range(n_sub):
                sel = value_tile(sel, m_sel, vst_ref[js], s_ref, js)
            osel_ref[...] = normalised(sel)

    osel_t = osel_ref[...]
    owin_t = owin_ref[...]

    gt = jnp.transpose(jax.nn.sigmoid(gsel_ref[...].astype(F32)))
    for j in range(hg):
        cs = slice(j * t, (j + 1) * t)
        ot = (gt[j:j + 1] * ocmp_t[:, cs] + gt[hg + j:hg + j + 1] * osel_t[:, cs]
              + gt[2 * hg + j:2 * hg + j + 1] * owin_t[:, cs])
        hs = slice(j * HEAD_DIM, (j + 1) * HEAD_DIM)
        o_ref[:, hs] = (jnp.transpose(ot)
                        * jax.nn.silu(bgate_ref[:, hs].astype(F32))).astype(o_ref.dtype)


def _attention_constants(seq):
    t = ATT_TILE
    n_cmp_pad = seq // CMP_STRIDE
    assert n_cmp_pad == t, "one tile of compressed keys per sequence"
    n_cmp = n_cmp_pad - CMP_BLOCK // CMP_STRIDE + 1
    n_sel = seq // SEL_BLOCK
    cst = np.arange(n_cmp_pad) * CMP_STRIDE
    sst = np.arange(n_sel) * SEL_BLOCK
    overlap = ((cst[:, None] < sst[None] + SEL_BLOCK) & (cst[:, None] + CMP_BLOCK > sst[None]))
    overlap[n_cmp:] = False
    ov = np.zeros((t, n_cmp_pad), np.float32)
    ov[:n_sel] = overlap.T
    keys = np.arange(seq)
    expand = np.zeros((seq, t), np.float32)
    expand[keys, keys // SEL_BLOCK] = NEG_INF
    eye = np.eye(t, dtype=np.float32)
    return (jnp.asarray(ov, BF16), jnp.asarray(expand, BF16), jnp.asarray(eye, BF16))


def _attention(z, kvc, bias_tile, bias_cmp, consts, bsz, seq):
    t = ATT_TILE
    kt = KEY_TILE
    n_q = seq // t
    n_kt = seq // kt
    gw = HEADS_PER_GROUP * HEAD_DIM
    kv0 = Z_KV // HEAD_DIM
    ov, expand, eye = consts

    def kv_spec(kind):
        return pl.BlockSpec((seq, HEAD_DIM),
                            lambda bg, i: (bg // KV_GROUPS, kv0 + kind * KV_GROUPS + bg % KV_GROUPS))

    def cmp_spec(kind):
        return pl.BlockSpec((None, None, None, t, HEAD_DIM),
                            lambda bg, i: (bg // KV_GROUPS, kind, bg % KV_GROUPS, 0, 0))

    def tok(bg, i):
        return (bg // KV_GROUPS) * n_q + i

    return pl.pallas_call(
        _attn_kernel,
        grid=(bsz * KV_GROUPS, n_q),
        in_specs=[
            pl.BlockSpec((t, gw), lambda bg, i: (tok(bg, i), Z_Q // gw + bg % KV_GROUPS)),
            kv_spec(KV_KINDS.index("k_s")), kv_spec(KV_KINDS.index("v_s")),
            kv_spec(KV_KINDS.index("k_w")), kv_spec(KV_KINDS.index("v_w")),
            cmp_spec(0), cmp_spec(1),
            pl.BlockSpec((t, V7X_LANES), lambda bg, i: (tok(bg, i), Z_GSEL // V7X_LANES + bg % KV_GROUPS)),
            pl.BlockSpec((t, gw), lambda bg, i: (tok(bg, i), Z_BG // gw + bg % KV_GROUPS)),
            pl.BlockSpec((None, N_BIAS_TILES, kt, gw), lambda bg, i: (bg % KV_GROUPS, 0, 0, 0)),
            pl.BlockSpec((None, None, t, gw), lambda bg, i: (bg % KV_GROUPS, i, 0, 0)),
            pl.BlockSpec((t, t), lambda bg, i: (0, 0)),
            pl.BlockSpec((seq, t), lambda bg, i: (0, 0)),
            pl.BlockSpec((t, t), lambda bg, i: (0, 0)),
        ],
        out_specs=pl.BlockSpec((t, gw), lambda bg, i: (tok(bg, i), bg % KV_GROUPS)),
        out_shape=jax.ShapeDtypeStruct((bsz * seq, D_MODEL), BF16),
        scratch_shapes=[pltpu.VMEM((n_kt, HEAD_DIM, kt), BF16),
                        pltpu.VMEM((n_kt, HEAD_DIM, kt), BF16),
                        pltpu.VMEM((seq, HEAD_DIM + t), BF16),
                        pltpu.VMEM((n_kt, kt, gw), F32),
                        pltpu.VMEM((WINDOW // kt + 1, kt, gw), F32),
                        pltpu.VMEM((HEAD_DIM, gw), F32),
                        pltpu.VMEM((HEAD_DIM, gw), F32)],
        compiler_params=_params("parallel", "arbitrary"),
        name="nsa_attention",
    )(z, z, z, z, z, kvc, kvc, z, z, bias_tile, bias_cmp, ov, expand, eye)


def _merge_kernel(ya_ref, yb_ref, yc_ref, ma_ref, mb_ref, mc_ref, w_ref, o_ref):
    y = jax.nn.sigmoid(ma_ref[...].astype(F32)) * _dot(ya_ref[...], w_ref[0])
    y += jax.nn.sigmoid(mb_ref[...].astype(F32)) * _dot(yb_ref[...], w_ref[1])
    y += jax.nn.sigmoid(mc_ref[...].astype(F32)) * _dot(yc_ref[...], w_ref[2])
    o_ref[...] = y.astype(o_ref.dtype)


def _merge(ya, yb, yc, zf, w_branch):
    m, d = ya.shape
    tm, tn = min(1024, m), 256
    m0 = Z_MERGE // tn
    per = d // tn
    y_spec = pl.BlockSpec((tm, d), lambda i, j: (i, 0))
    return pl.pallas_call(
        _merge_kernel,
        grid=(m // tm, d // tn),
        in_specs=[
            y_spec, y_spec, y_spec,
            pl.BlockSpec((tm, tn), lambda i, j: (i, m0 + j)),
            pl.BlockSpec((tm, tn), lambda i, j: (i, m0 + per + j)),
            pl.BlockSpec((tm, tn), lambda i, j: (i, m0 + 2 * per + j)),
            pl.BlockSpec((3, d, tn), lambda i, j: (0, 0, j)),
        ],
        out_specs=pl.BlockSpec((tm, tn), lambda i, j: (i, j)),
        out_shape=jax.ShapeDtypeStruct((m, d), BF16),
        compiler_params=_params("parallel", "arbitrary"),
        name="merge",
    )(ya, yb, yc, zf, zf, zf, w_branch)


def _out_kernel(y_ref, w_ref, x_ref, mod_ref, g_ref, *rest, last):
    x = x_ref[...] + mod_ref[2:3, :] * _dot(y_ref[...], w_ref[...])
    normed = x * lax.rsqrt(jnp.mean(x * x, axis=-1, keepdims=True) + EPS) * g_ref[...]
    if last:
        (o_ref,) = rest
        o_ref[...] = normed
    else:
        next_mod_ref, x_out_ref, h_ref = rest
        nm = next_mod_ref[...]
        x_out_ref[...] = x
        h_ref[...] = (normed * (1.0 + nm[1:2]) + nm[0:1]).astype(h_ref.dtype)


def _out_proj(y, w_out, x2, mod3, g, next_mod3, seq):
    m, d = y.shape
    tm = 512
    per_seq = seq // tm
    last = next_mod3 is None
    row = pl.BlockSpec((tm, d), lambda i: (i, 0))
    mod_spec = pl.BlockSpec((None, 3, d), lambda i: (i // per_seq, 0, 0))
    in_specs = [row, pl.BlockSpec((d, d), lambda i: (0, 0)), row, mod_spec,
                pl.BlockSpec((1, d), lambda i: (0, 0))]
    args = [y, w_out, x2, mod3, g.reshape(1, d)]
    if last:
        out_specs = row
        out_shape = jax.ShapeDtypeStruct((m, d), F32)
    else:
        in_specs.append(mod_spec)
        args.append(next_mod3)
        out_specs = [row, row]
        out_shape = [jax.ShapeDtypeStruct((m, d), F32), jax.ShapeDtypeStruct((m, d), BF16)]
    return pl.pallas_call(
        functools.partial(_out_kernel, last=last),
        grid=(m // tm,),
        in_specs=in_specs,
        out_specs=out_specs,
        out_shape=out_shape,
        compiler_params=_params("parallel"),
        name="out_proj",
    )(*args)


PREP_BLOCK = 512
N_GSEL = 3 * N_HEADS
_PREP_SEGMENTS = (
    (Z_Q, 3 * D_MODEL, D_MODEL, 0),
    (Z_BG, 4 * D_MODEL + 6 * KV_GROUPS * HEAD_DIM + N_GSEL, 6 * D_MODEL, 1),
    (Z_KV, 4 * D_MODEL, 6 * KV_GROUPS * HEAD_DIM, 0),
    (Z_GSEL, 4 * D_MODEL + 6 * KV_GROUPS * HEAD_DIM, KV_GROUPS * V7X_LANES, 2),
    (W_A, 0, 3 * D_MODEL, 0),
)


def _prep_lookup(j, per_segment):
    out = jnp.int32(0)
    for seg in _PREP_SEGMENTS:
        first = seg[0] // PREP_BLOCK
        out = jnp.where(j >= first, per_segment(seg, j - first), out)
    return out


def _prep_src_block(j):
    def src(seg, k):
        _, src_col, _, placement = seg
        base = (src_col - (N_GSEL if placement == 1 else 0)) // PREP_BLOCK
        return base if placement == 2 else base + k
    return _prep_lookup(j, src)


def _wprep_kernel(a_ref, b_ref, m_ref, o_ref, *, q_scale):
    j = pl.program_id(1)
    kind = _prep_lookup(j, lambda seg, k: seg[3])
    lane = lax.broadcasted_iota(jnp.int32, b_ref.shape, 1)
    tail = jnp.where(lane < jnp.where(kind == 1, N_GSEL, 0), b_ref[...], 0.0)
    scale = jnp.where(j < D_MODEL // PREP_BLOCK, q_scale, 1.0).astype(F32)
    x = (jnp.concatenate([a_ref[...], tail], axis=1) * scale).astype(BF16)
    o_ref[...] = _dot(x, m_ref[kind]).astype(o_ref.dtype)


def _prep_weights(w_in):
    depth, d, n_in = w_in.shape
    pb = PREP_BLOCK
    assert Z_Q == 0 and all(seg[0] % pb == 0 and seg[2] % pb == 0 for seg in _PREP_SEGMENTS)
    rows = pb + V7X_LANES
    place = np.zeros((3, rows, pb), np.float32)
    place[0, np.arange(pb), np.arange(pb)] = 1.0
    place[1, np.arange(pb) + N_GSEL, np.arange(pb)] = 1.0
    for br in range(3):
        for g in range(KV_GROUPS):
            for jh in range(HEADS_PER_GROUP):
                place[2, br * N_HEADS + g * HEADS_PER_GROUP + jh,
                      g * V7X_LANES + br * HEADS_PER_GROUP + jh] = 1.0
    return pl.pallas_call(
        functools.partial(_wprep_kernel, q_scale=HEAD_DIM ** -0.5 * LOG2_E),
        grid=(depth, W_COLS // pb),
        in_specs=[
            pl.BlockSpec((None, d, pb), lambda l, j: (l, 0, _prep_src_block(j))),
            pl.BlockSpec((None, d, V7X_LANES),
                         lambda l, j: (l, 0, (_prep_src_block(j) + 1) * (pb // V7X_LANES))),
            pl.BlockSpec((3, rows, pb), lambda l, j: (0, 0, 0)),
        ],
        out_specs=pl.BlockSpec((None, d, pb), lambda l, j: (l, 0, j)),
        out_shape=jax.ShapeDtypeStruct((depth, d, W_COLS), BF16),
        compiler_params=_params("parallel", "parallel"),
        name="weight_prep",
    )(w_in, w_in, jnp.asarray(place, BF16))


def kernel(x, c, rel_bias, norm_g, w_ada, b_ada, w_in, a_ln_g, a_ln_b, a_w_s, a_b_s, b_w_cmp1,
           b_w_cmp2, b_pos_cmp, c_w_grp, c_scale, w_branch, w_out, final_g):
    bsz, seq, d = x.shape
    depth = w_in.shape[0]
    assert d == D_MODEL and seq % 512 == 0
    x2 = x.reshape(bsz * seq, d)
    mod = _ada(c, w_ada, b_ada)
    bias_tile, bias_cmp = _bias_tables(rel_bias, seq)
    consts = _attention_constants(seq)
    mod3 = [mod[l].reshape(bsz, 3, d) for l in range(depth)]
    h = _prenorm(x2, norm_g[0], mod3[0], seq)
    w_all = _prep_weights(w_in)
    for l in range(depth):
        z = _proj(h, w_all, l, Z_COLS, BF16)
        ya = _proj_sgu(h, w_all, l, a_ln_g[l], a_ln_b[l], a_w_s[l], a_b_s[l])
        yc = _pool(z, c_w_grp[l], c_scale[l], seq)
        kvc = _compress(z, b_w_cmp1[l], b_w_cmp2[l], b_pos_cmp[l], bsz, seq)
        yb = _attention(z, kvc, bias_tile, bias_cmp, consts, bsz, seq)
        y = _merge(ya, yb, yc, z, w_branch[l].astype(BF16))
        if l + 1 < depth:
            x2, h = _out_proj(y, w_out[l].astype(BF16), x2, mod3[l], norm_g[l + 1], mod3[l + 1], seq)
        else:
            out = _out_proj(y, w_out[l].astype(BF16), x2, mod3[l], final_g, None, seq)
    return out.reshape(bsz, seq, d)
```

```python
import functools
import math

import numpy as np
import jax
import jax.numpy as jnp
from jax import lax
from jax.experimental import pallas as pl
from jax.experimental.pallas import tpu as pltpu

D_MODEL = 2048
A_GROUPS = 8
A_CHUNK = 128
HEAD_DIM = 128
N_HEADS = 16
KV_GROUPS = 4
HEADS_PER_GROUP = N_HEADS // KV_GROUPS
CMP_BLOCK = 32
CMP_STRIDE = 16
SEL_BLOCK = 64
N_SELECT = 16
WINDOW = 512
FORCE_BONUS = 1e4
POOL_WINDOWS = (2, 4, 8, 16)
C_GROUP = D_MODEL // len(POOL_WINDOWS)
REL_BUCKETS = 32
REL_MAX_DIST = 128
EPS = 1e-6
NEG_INF = -1e30

V7X_LANES = 128
V7X_VMEM_BYTES = 64 * 1024 * 1024
VMEM_LIMIT = 48 * 1024 * 1024

ATT_TILE = 128
KEY_TILE = 256
LOG2_E = math.log2(math.e)
POOL_HALO = 16
BF16 = jnp.bfloat16
F32 = jnp.float32

Z_Q = 0
Z_BG = 2048
Z_CX, Z_CG = 4096, 6144
Z_MERGE = 8192
Z_KV = 14336
Z_GSEL = 17408
Z_COLS = 17920
W_A = Z_COLS
W_COLS = W_A + 3 * D_MODEL
KV_KINDS = ("k_c", "v_c", "k_s", "v_s", "k_w", "v_w")


def _dot(a, b):
    return jnp.dot(a, b, preferred_element_type=F32)


def _dot_nt(a, b):
    return lax.dot_general(a, b, (((1,), (1,)), ((), ())), preferred_element_type=F32)


def _params(*sem):
    return pltpu.CompilerParams(dimension_semantics=sem, vmem_limit_bytes=VMEM_LIMIT)


def _ada_kernel(c_ref, w_ref, b_ref, o_ref):
    a = jax.nn.silu(c_ref[...]).astype(BF16)
    o_ref[...] = _dot(a, w_ref[...].astype(BF16)) + b_ref[...]


def _ada(c, w_ada, b_ada):
    depth, d, n = w_ada.shape
    bsz = c.shape[0]
    tn = 1024
    return pl.pallas_call(
        _ada_kernel,
        grid=(depth, n // tn),
        in_specs=[
            pl.BlockSpec((bsz, d), lambda l, j: (0, 0)),
            pl.BlockSpec((None, d, tn), lambda l, j: (l, 0, j)),
            pl.BlockSpec((None, 1, tn), lambda l, j: (l, 0, j)),
        ],
        out_specs=pl.BlockSpec((None, bsz, tn), lambda l, j: (l, 0, j)),
        out_shape=jax.ShapeDtypeStruct((depth, bsz, n), F32),
        compiler_params=_params("parallel", "parallel"),
        name="ada_mod",
    )(c, w_ada, b_ada.reshape(depth, 1, n))


def _prenorm_kernel(x_ref, g_ref, mod_ref, o_ref):
    x = x_ref[...]
    y = x * lax.rsqrt(jnp.mean(x * x, axis=-1, keepdims=True) + EPS) * g_ref[...]
    m = mod_ref[...]
    o_ref[...] = (y * (1.0 + m[1:2]) + m[0:1]).astype(o_ref.dtype)


def _prenorm(x2, g, mod3, seq):
    m, d = x2.shape
    tm = 512
    per_seq = seq // tm
    return pl.pallas_call(
        _prenorm_kernel,
        grid=(m // tm,),
        in_specs=[
            pl.BlockSpec((tm, d), lambda i: (i, 0)),
            pl.BlockSpec((1, d), lambda i: (0, 0)),
            pl.BlockSpec((None, 3, d), lambda i: (i // per_seq, 0, 0)),
        ],
        out_specs=pl.BlockSpec((tm, d), lambda i: (i, 0)),
        out_shape=jax.ShapeDtypeStruct((m, d), BF16),
        compiler_params=_params("parallel"),
        name="prenorm",
    )(x2, g.reshape(1, d), mod3)


def _proj_kernel(a_ref, w_ref, o_ref):
    o_ref[...] = _dot_nt(a_ref[...], w_ref[...]).astype(o_ref.dtype)


def _proj(a, w_all, layer, n, out_dtype, tm=1024, tn=1280):
    m, k = a.shape
    tm = min(tm, m)
    return pl.pallas_call(
        _proj_kernel,
        grid=(m // tm, n // tn),
        in_specs=[
            pl.BlockSpec((tm, k), lambda i, j: (i, 0)),
            pl.BlockSpec((None, tn, k), lambda i, j: (layer, j, 0)),
        ],
        out_specs=pl.BlockSpec((tm, tn), lambda i, j: (i, j)),
        out_shape=jax.ShapeDtypeStruct((m, n), out_dtype),
        compiler_params=_params("parallel", "arbitrary"),
        name="in_proj",
    )(a, w_all)


def _proj_sgu_kernel(h_ref, wu_ref, wv_ref, wg_ref, lng_ref, lnb_ref, ws_ref, bst_ref, o_ref,
                     v_buf, p_buf):
    j = pl.program_id(1)
    n_j, tm, tn = v_buf.shape
    cg = D_MODEL // A_GROUPS
    h = h_ref[...]
    v_buf[j] = jax.nn.gelu(_dot_nt(h, wv_ref[...]))
    p_buf[j] = jax.nn.gelu(_dot_nt(h, wu_ref[...])) * jax.nn.silu(_dot_nt(h, wg_ref[...]))

    @pl.when(j == n_j - 1)
    def _():
        total = jnp.sum(v_buf[0], axis=-1, keepdims=True)
        for jj in range(1, n_j):
            total = total + jnp.sum(v_buf[jj], axis=-1, keepdims=True)
        mu = total * (1.0 / D_MODEL)
        sq = jnp.zeros_like(mu)
        for jj in range(n_j):
            dv = v_buf[jj] - mu
            sq = sq + jnp.sum(dv * dv, axis=-1, keepdims=True)
        rstd = lax.rsqrt(sq * (1.0 / D_MODEL) + EPS)
        row = lax.broadcasted_iota(jnp.int32, (A_CHUNK, A_CHUNK), 0)
        col = lax.broadcasted_iota(jnp.int32, (A_CHUNK, A_CHUNK), 1)
        tril = row >= col
        bst = bst_ref[...]
        for g in range(A_GROUPS):
            ws = jnp.where(tril, ws_ref[g], 0.0).astype(BF16)
            jj, off = divmod(g * cg, tn)
            cs = slice(g * cg, (g + 1) * cg)
            for c in range(tm // A_CHUNK):
                rs = slice(c * A_CHUNK, (c + 1) * A_CHUNK)
                vn = ((v_buf[jj, rs, off:off + cg] - mu[rs]) * rstd[rs] * lng_ref[:, cs]
                      + lnb_ref[:, cs]).astype(BF16)
                mixed = _dot(ws, vn) + bst[:, g:g + 1]
                o_ref[rs, cs] = (p_buf[jj, rs, off:off + cg] * mixed).astype(o_ref.dtype)


def _proj_sgu(h, w_all, layer, ln_g, ln_b, w_s, b_s):
    m, d = h.shape
    tm, tn = 512, 512
    per = d // tn
    a0 = W_A // tn
    return pl.pallas_call(
        _proj_sgu_kernel,
        grid=(m // tm, per),
        in_specs=[
            pl.BlockSpec((tm, d), lambda i, j: (i, 0)),
            pl.BlockSpec((None, tn, d), lambda i, j: (layer, a0 + j, 0)),
            pl.BlockSpec((None, tn, d), lambda i, j: (layer, a0 + per + j, 0)),
            pl.BlockSpec((None, tn, d), lambda i, j: (layer, a0 + 2 * per + j, 0)),
            pl.BlockSpec((1, d), lambda i, j: (0, 0)),
            pl.BlockSpec((1, d), lambda i, j: (0, 0)),
            pl.BlockSpec((A_GROUPS, A_CHUNK, A_CHUNK), lambda i, j: (0, 0, 0)),
            pl.BlockSpec((A_CHUNK, A_GROUPS), lambda i, j: (0, 0)),
        ],
        out_specs=pl.BlockSpec((tm, d), lambda i, j: (i, 0)),
        out_shape=jax.ShapeDtypeStruct((m, d), BF16),
        scratch_shapes=[pltpu.VMEM((per, tm, tn), F32), pltpu.VMEM((per, tm, tn), F32)],
        compiler_params=_params("parallel", "arbitrary"),
        name="proj_sgu",
    )(h, w_all, w_all, w_all, ln_g.reshape(1, d), ln_b.reshape(1, d), w_s, b_s.T)


def _pool_kernel(x_ref, halo_ref, gate_ref, wg_ref, ls_ref, o_ref, ext_ref, *, tiles_per_seq):
    tm = x_ref.shape[0]
    i = pl.program_id(0)
    pos0 = (i % tiles_per_seq) * tm
    halo = halo_ref[...].astype(F32)
    ext_ref[0:POOL_HALO, :] = jnp.where(pos0 == 0, jnp.zeros_like(halo), halo)
    ext_ref[POOL_HALO:POOL_HALO + tm, :] = x_ref[...].astype(F32)
    tpos = pos0 + lax.broadcasted_iota(jnp.int32, (tm, 1), 0)
    for gi, w in enumerate(POOL_WINDOWS):
        cs = slice(gi * C_GROUP, (gi + 1) * C_GROUP)
        x = ext_ref[POOL_HALO:POOL_HALO + tm, cs]
        s = x
        for back in range(1, w):
            s = s + ext_ref[POOL_HALO - back:POOL_HALO - back + tm, cs]
        cnt = jnp.minimum(tpos + 1, w).astype(F32)
        y = (s / cnt - x).astype(BF16)
        yg = (_dot(y, wg_ref[gi].astype(BF16)) * ls_ref[:, cs]
              * jax.nn.silu(gate_ref[:, cs].astype(F32)))
        o_ref[:, cs] = yg.astype(o_ref.dtype)


def _pool(zf, w_grp, ls, seq):
    m = zf.shape[0]
    tm = 256
    d = D_MODEL
    halo_blocks = tm // POOL_HALO
    return pl.pallas_call(
        functools.partial(_pool_kernel, tiles_per_seq=seq // tm),
        grid=(m // tm,),
        in_specs=[
            pl.BlockSpec((tm, d), lambda i: (i, Z_CX // d)),
            pl.BlockSpec((POOL_HALO, d), lambda i: (jnp.maximum(i * halo_blocks - 1, 0), Z_CX // d)),
            pl.BlockSpec((tm, d), lambda i: (i, Z_CG // d)),
            pl.BlockSpec((len(POOL_WINDOWS), C_GROUP, C_GROUP), lambda i: (0, 0, 0)),
            pl.BlockSpec((1, d), lambda i: (0, 0)),
        ],
        out_specs=pl.BlockSpec((tm, d), lambda i: (i, 0)),
        out_shape=jax.ShapeDtypeStruct((m, d), BF16),
        scratch_shapes=[pltpu.VMEM((POOL_HALO + tm, d), F32)],
        compiler_params=_params("parallel"),
        name="pool",
    )(zf, zf, zf, w_grp, ls.reshape(1, d))


def _compress_kernel(x_ref, w1_ref, w2_ref, pos_ref, o_ref, xf_ref):
    n_chunks = x_ref.shape[0] // CMP_STRIDE
    xf_ref[...] = x_ref[...].astype(F32)
    pos = pos_ref[...]
    first = jnp.zeros((n_chunks, HEAD_DIM), F32)
    second = jnp.zeros((n_chunks, HEAD_DIM), F32)
    for l in range(CMP_STRIDE):
        xl = xf_ref[pl.ds(l, n_chunks, stride=CMP_STRIDE), :]
        first += _dot((xl + pos[l:l + 1]).astype(BF16), w1_ref[l].astype(BF16))
        second += _dot((xl + pos[CMP_STRIDE + l:CMP_STRIDE + l + 1]).astype(BF16),
                       w1_ref[CMP_STRIDE + l].astype(BF16))
    hdn = jax.nn.gelu(first + pltpu.roll(second, n_chunks - 1, axis=0))
    o_ref[...] = _dot(hdn.astype(BF16), w2_ref[...].astype(BF16)).astype(o_ref.dtype)


def _compress(z, w1, w2, pos, bsz, seq):
    n_chunks = seq // CMP_STRIDE
    kv0 = Z_KV // HEAD_DIM
    return pl.pallas_call(
        _compress_kernel,
        grid=(bsz, 2, KV_GROUPS),
        in_specs=[
            pl.BlockSpec((seq, HEAD_DIM), lambda b, t, g: (b, kv0 + t * KV_GROUPS + g)),
            pl.BlockSpec((None, CMP_BLOCK, HEAD_DIM, HEAD_DIM), lambda b, t, g: (t, 0, 0, 0)),
            pl.BlockSpec((None, HEAD_DIM, HEAD_DIM), lambda b, t, g: (t, 0, 0)),
            pl.BlockSpec((None, CMP_BLOCK, HEAD_DIM), lambda b, t, g: (t, 0, 0)),
        ],
        out_specs=pl.BlockSpec((None, None, None, n_chunks, HEAD_DIM), lambda b, t, g: (b, t, g, 0, 0)),
        out_shape=jax.ShapeDtypeStruct((bsz, 2, KV_GROUPS, n_chunks, HEAD_DIM), BF16),
        scratch_shapes=[pltpu.VMEM((seq, HEAD_DIM), F32)],
        compiler_params=_params("parallel", "parallel", "parallel"),
        name="compress",
    )(z, w1, w2, pos)


def _bucket_of_distance(n):
    n = np.asarray(n, np.int64)
    max_exact = REL_BUCKETS // 2
    nf = np.maximum(n, 1).astype(np.float64)
    large = max_exact + (np.log(nf / max_exact) / math.log(REL_MAX_DIST / max_exact)
                         * (REL_BUCKETS - max_exact)).astype(np.int64)
    large = np.minimum(large, REL_BUCKETS - 1)
    return np.where(n < max_exact, n, large).astype(np.int32)


def _bias_kernel(tab_ref, idx_tile_ref, idx_cmp_ref, tile_ref, cmp_ref):
    h = pl.program_id(0) * HEADS_PER_GROUP + pl.program_id(1)
    far = tab_ref[REL_BUCKETS - 1, h]
    for idx_ref, out_ref, shift in ((idx_tile_ref, tile_ref, far), (idx_cmp_ref, cmp_ref, 0.0)):
        idx = idx_ref[...]
        acc = jnp.full(idx.shape, NEG_INF, F32)
        for b in range(REL_BUCKETS):
            acc = jnp.where(idx == b, (tab_ref[b, h] - shift) * LOG2_E, acc)
        out_ref[...] = acc


N_BIAS_TILES = 7
MASKED_TILE = N_BIAS_TILES - 1


def _bias_tables(rel_bias, seq):
    t = ATT_TILE
    kt = KEY_TILE
    n_q = seq // t
    r = np.arange(t)[None, :]
    masked = REL_BUCKETS

    def tile(offset, windowed):
        d = offset + r - np.arange(kt)[:, None]
        ok = (d >= 0) & ((d < WINDOW) | (not windowed))
        return np.where(ok, _bucket_of_distance(np.maximum(d, 0)), masked)

    idx_tile = np.stack([tile(0, False), tile(t, False), tile(2 * t, False), tile(3 * t, False),
                         tile(4 * t, True), tile(5 * t, True),
                         np.full((kt, t), masked)]).astype(np.int32)
    assert idx_tile.shape[0] == N_BIAS_TILES and (idx_tile[3] == REL_BUCKETS - 1).all()
    assert (tile(3 * t, True) == idx_tile[3]).all()
    cmp_end = np.arange(t)[:, None] * CMP_STRIDE + CMP_BLOCK - 1
    idx_cmp = np.stack([_bucket_of_distance(np.maximum(i * t + r - cmp_end, 0))
                        for i in range(n_q)]).astype(np.int32)
    return pl.pallas_call(
        _bias_kernel,
        grid=(KV_GROUPS, HEADS_PER_GROUP),
        in_specs=[
            pl.BlockSpec(memory_space=pltpu.SMEM),
            pl.BlockSpec((N_BIAS_TILES, kt, t), lambda g, j: (0, 0, 0)),
            pl.BlockSpec((n_q, t, t), lambda g, j: (0, 0, 0)),
        ],
        out_specs=[
            pl.BlockSpec((None, N_BIAS_TILES, kt, t), lambda g, j: (g, 0, 0, j)),
            pl.BlockSpec((None, n_q, t, t), lambda g, j: (g, 0, 0, j)),
        ],
        out_shape=[
            jax.ShapeDtypeStruct((KV_GROUPS, N_BIAS_TILES, kt, HEADS_PER_GROUP * t), F32),
            jax.ShapeDtypeStruct((KV_GROUPS, n_q, t, HEADS_PER_GROUP * t), F32),
        ],
        compiler_params=_params("parallel", "parallel"),
        name="bias_tables",
    )(rel_bias, jnp.asarray(idx_tile), jnp.asarray(idx_cmp))


def _attn_kernel(q_ref, ks_ref, vs_ref, kw_ref, vw_ref, kc_ref, vc_ref, gsel_ref, bgate_ref,
                 btile_ref, bcmp_ref, ov_ref, expand_ref, eye_ref, o_ref,
                 vst_ref, vwt_ref, kaug_ref, s_ref, w_ref, osel_ref, owin_ref):
    t = ATT_TILE
    kt = KEY_TILE
    hg = HEADS_PER_GROUP
    cols = hg * t
    n_kt = s_ref.shape[0]
    n_q = n_kt * kt // t
    i = pl.program_id(1)
    last_tile = lax.shift_right_logical(i, 1)
    eye = eye_ref[...]

    @pl.when(i == 0)
    def _():
        kaug_ref[:, 0:HEAD_DIM] = ks_ref[...]
        kaug_ref[:, HEAD_DIM:] = expand_ref[...]
        for jt in range(n_kt):
            ks = slice(jt * kt, (jt + 1) * kt)
            vst_ref[jt] = _dot_nt(eye, vs_ref[ks, :]).astype(BF16)
            vwt_ref[jt] = _dot_nt(eye, vw_ref[ks, :]).astype(BF16)

    qb = q_ref[...]
    qs = jnp.concatenate([qb[:, j * HEAD_DIM:(j + 1) * HEAD_DIM] for j in range(hg)], axis=0)

    def bias_index(delta):
        return jnp.where(delta < 0, MASKED_TILE, delta)

    win_tiles = WINDOW // kt + 1
    win_first = jnp.maximum(last_tile - (win_tiles - 1), 0)

    def win_score_tile(js):
        jt = win_first + js
        k = kw_ref[pl.ds(pl.multiple_of(jt * kt, kt), kt), :]
        st = _dot_nt(k, qs) + btile_ref[bias_index(i - 2 * jt)]
        w_ref[js] = st
        return st

    n_idx = lax.broadcasted_iota(jnp.int32, (t, cols), 0)
    tq = i * t + (lax.broadcasted_iota(jnp.int32, (t, cols), 1) & (t - 1))
    valid = tq >= n_idx * CMP_STRIDE + (CMP_BLOCK - 1)
    s = jnp.where(valid, _dot_nt(kc_ref[...], qs) + bcmp_ref[...], NEG_INF)
    win_mx = win_score_tile(0)
    e = jnp.where(valid, jnp.exp2(s - jnp.max(s, axis=0, keepdims=True)), 0.0)
    l = jnp.sum(e, axis=0, keepdims=True)
    p_c = e * (1.0 / jnp.where(l > 0.0, l, 1.0))
    vct = _dot_nt(eye, vc_ref[...]).astype(BF16)
    ocmp_t = _dot(vct, p_c.astype(BF16))

    psum = p_c[:, 0:t]
    for j in range(1, hg):
        psum = psum + p_c[:, j * t:(j + 1) * t]
    p_hi = psum.astype(BF16)
    p_lo = (psum - p_hi.astype(F32)).astype(BF16)
    ov = ov_ref[...]
    n_sel = n_kt * kt // SEL_BLOCK
    imp = (_dot(ov, p_hi) + _dot(ov, p_lo))[0:n_sel]
    win_mx = jnp.maximum(win_mx, win_score_tile(1))
    blk = lax.broadcasted_iota(jnp.int32, (n_sel, t), 0)
    tqb = i * t + lax.broadcasted_iota(jnp.int32, (n_sel, t), 1)
    cur = lax.shift_right_logical(tqb, int(math.log2(SEL_BLOCK)))
    forced = (blk == 0) | (blk == cur) | (blk == cur - 1)
    future = blk * SEL_BLOCK > tqb
    imp = jnp.where(future, -1.0, imp + jnp.where(forced, FORCE_BONUS, 0.0))
    rank = jnp.zeros((n_sel, t), F32)
    for sp in range(n_sel):
        other = imp[sp:sp + 1, :]
        beats = (other > imp) | ((other == imp) & (blk > sp))
        rank = rank + jnp.where(beats, 1.0, 0.0)
    unsel = jnp.where(rank < float(min(N_SELECT, n_sel)), 0.0, 1.0)
    unsel_pad = jnp.concatenate([unsel, jnp.zeros((t - n_sel, t), F32)], axis=0).astype(BF16)
    unsel_qb = _dot_nt(eye, unsel_pad).astype(BF16)
    for js in range(2, win_tiles):
        win_mx = jnp.maximum(win_mx, win_score_tile(js))
    m_win = jnp.max(win_mx, axis=0, keepdims=True)
    q_aug = jnp.concatenate([qs, jnp.concatenate([unsel_qb] * hg, axis=0)], axis=1)

    def value_tile(carry, m, vt, sc_ref, js):
        lsum, acc = carry
        p = jnp.exp2(sc_ref[js] - m)
        return lsum + p, acc + _dot(vt, p.astype(BF16))

    def normalised(carry):
        lsum, acc = carry
        return acc * (1.0 / jnp.sum(lsum, axis=0, keepdims=True))

    zero_carry = (jnp.zeros((kt, cols), F32), jnp.zeros((HEAD_DIM, cols), F32))

    q_per_variant = kt // t
    far_offset = 3
    for variant in range(n_q // q_per_variant):
        n_sub = variant + 1
        n_far = max((q_per_variant * variant - far_offset) // 2 + 1, 0)
        first = max(variant - (win_tiles - 1), 0)

        def sel_bias(js, n_far=n_far):
            if js < n_far:
                return None
            return btile_ref[bias_index(jnp.minimum(i - 2 * js, far_offset))]

        @pl.when(last_tile == variant)
        def _(n_sub=n_sub, sel_bias=sel_bias, first=first):
            mx = jnp.full((kt, cols), NEG_INF, F32)
            for js in range(n_sub):
                st = _dot_nt(kaug_ref[js * kt:(js + 1) * kt, :], q_aug)
                bias = sel_bias(js)
                if bias is not None:
                    st = st + bias
                s_ref[js] = st
                mx = jnp.maximum(mx, st)
            win = zero_carry
            for js in range(win_tiles):
                win = value_tile(win, m_win, vwt_ref[first + js], w_ref, js)
            owin_ref[...] = normalised(win)
            m_sel = jnp.max(mx, axis=0, keepdims=True)
            sel = zero_carry
            for js in range(n_sub):
                sel = value_tile(sel, m_sel, vst_ref[js], s_ref, js)
            osel_ref[...] = normalised(sel)

    osel_t = osel_ref[...]
    owin_t = owin_ref[...]

    gt = jnp.transpose(jax.nn.sigmoid(gsel_ref[...].astype(F32)))
    for j in range(hg):
        cs = slice(j * t, (j + 1) * t)
        ot = (gt[j:j + 1] * ocmp_t[:, cs] + gt[hg + j:hg + j + 1] * osel_t[:, cs]
              + gt[2 * hg + j:2 * hg + j + 1] * owin_t[:, cs])
        hs = slice(j * HEAD_DIM, (j + 1) * HEAD_DIM)
        o_ref[:, hs] = (jnp.transpose(ot)
                        * jax.nn.silu(bgate_ref[:, hs].astype(F32))).astype(o_ref.dtype)


def _attention_constants(seq):
    t = ATT_TILE
    n_cmp_pad = seq // CMP_STRIDE
    assert n_cmp_pad == t, "one tile of compressed keys per sequence"
    n_cmp = n_cmp_pad - CMP_BLOCK // CMP_STRIDE + 1
    n_sel = seq // SEL_BLOCK
    cst = np.arange(n_cmp_pad) * CMP_STRIDE
    sst = np.arange(n_sel) * SEL_BLOCK
    overlap = ((cst[:, None] < sst[None] + SEL_BLOCK) & (cst[:, None] + CMP_BLOCK > sst[None]))
    overlap[n_cmp:] = False
    ov = np.zeros((t, n_cmp_pad), np.float32)
    ov[:n_sel] = overlap.T
    keys = np.arange(seq)
    expand = np.zeros((seq, t), np.float32)
    expand[keys, keys // SEL_BLOCK] = NEG_INF
    eye = np.eye(t, dtype=np.float32)
    return (jnp.asarray(ov, BF16), jnp.asarray(expand, BF16), jnp.asarray(eye, BF16))


def _attention(z, kvc, bias_tile, bias_cmp, consts, bsz, seq):
    t = ATT_TILE
    kt = KEY_TILE
    n_q = seq // t
    n_kt = seq // kt
    gw = HEADS_PER_GROUP * HEAD_DIM
    kv0 = Z_KV // HEAD_DIM
    ov, expand, eye = consts

    def kv_spec(kind):
        return pl.BlockSpec((seq, HEAD_DIM),
                            lambda bg, i: (bg // KV_GROUPS, kv0 + kind * KV_GROUPS + bg % KV_GROUPS))

    def cmp_spec(kind):
        return pl.BlockSpec((None, None, None, t, HEAD_DIM),
                            lambda bg, i: (bg // KV_GROUPS, kind, bg % KV_GROUPS, 0, 0))

    def tok(bg, i):
        return (bg // KV_GROUPS) * n_q + i

    return pl.pallas_call(
        _attn_kernel,
        grid=(bsz * KV_GROUPS, n_q),
        in_specs=[
            pl.BlockSpec((t, gw), lambda bg, i: (tok(bg, i), Z_Q // gw + bg % KV_GROUPS)),
            kv_spec(KV_KINDS.index("k_s")), kv_spec(KV_KINDS.index("v_s")),
            kv_spec(KV_KINDS.index("k_w")), kv_spec(KV_KINDS.index("v_w")),
            cmp_spec(0), cmp_spec(1),
            pl.BlockSpec((t, V7X_LANES), lambda bg, i: (tok(bg, i), Z_GSEL // V7X_LANES + bg % KV_GROUPS)),
            pl.BlockSpec((t, gw), lambda bg, i: (tok(bg, i), Z_BG // gw + bg % KV_GROUPS)),
            pl.BlockSpec((None, N_BIAS_TILES, kt, gw), lambda bg, i: (bg % KV_GROUPS, 0, 0, 0)),
            pl.BlockSpec((None, None, t, gw), lambda bg, i: (bg % KV_GROUPS, i, 0, 0)),
            pl.BlockSpec((t, t), lambda bg, i: (0, 0)),
            pl.BlockSpec((seq, t), lambda bg, i: (0, 0)),
            pl.BlockSpec((t, t), lambda bg, i: (0, 0)),
        ],
        out_specs=pl.BlockSpec((t, gw), lambda bg, i: (tok(bg, i), bg % KV_GROUPS)),
        out_shape=jax.ShapeDtypeStruct((bsz * seq, D_MODEL), BF16),
        scratch_shapes=[pltpu.VMEM((n_kt, HEAD_DIM, kt), BF16),
                        pltpu.VMEM((n_kt, HEAD_DIM, kt), BF16),
                        pltpu.VMEM((seq, HEAD_DIM + t), BF16),
                        pltpu.VMEM((n_kt, kt, gw), F32),
                        pltpu.VMEM((WINDOW // kt + 1, kt, gw), F32),
                        pltpu.VMEM((HEAD_DIM, gw), F32),
                        pltpu.VMEM((HEAD_DIM, gw), F32)],
        compiler_params=_params("parallel", "arbitrary"),
        name="nsa_attention",
    )(z, z, z, z, z, kvc, kvc, z, z, bias_tile, bias_cmp, ov, expand, eye)


def _merge_kernel(ya_ref, yb_ref, yc_ref, ma_ref, mb_ref, mc_ref, w_ref, o_ref):
    y = jax.nn.sigmoid(ma_ref[...].astype(F32)) * _dot(ya_ref[...], w_ref[0])
    y += jax.nn.sigmoid(mb_ref[...].astype(F32)) * _dot(yb_ref[...], w_ref[1])
    y += jax.nn.sigmoid(mc_ref[...].astype(F32)) * _dot(yc_ref[...], w_ref[2])
    o_ref[...] = y.astype(o_ref.dtype)


def _merge(ya, yb, yc, zf, w_branch):
    m, d = ya.shape
    tm, tn = min(1024, m), 256
    m0 = Z_MERGE // tn
    per = d // tn
    y_spec = pl.BlockSpec((tm, d), lambda i, j: (i, 0))
    return pl.pallas_call(
        _merge_kernel,
        grid=(m // tm, d // tn),
        in_specs=[
            y_spec, y_spec, y_spec,
            pl.BlockSpec((tm, tn), lambda i, j: (i, m0 + j)),
            pl.BlockSpec((tm, tn), lambda i, j: (i, m0 + per + j)),
            pl.BlockSpec((tm, tn), lambda i, j: (i, m0 + 2 * per + j)),
            pl.BlockSpec((3, d, tn), lambda i, j: (0, 0, j)),
        ],
        out_specs=pl.BlockSpec((tm, tn), lambda i, j: (i, j)),
        out_shape=jax.ShapeDtypeStruct((m, d), BF16),
        compiler_params=_params("parallel", "arbitrary"),
        name="merge",
    )(ya, yb, yc, zf, zf, zf, w_branch)


def _out_kernel(y_ref, w_ref, x_ref, mod_ref, g_ref, *rest, last):
    x = x_ref[...] + mod_ref[2:3, :] * _dot(y_ref[...], w_ref[...])
    normed = x * lax.rsqrt(jnp.mean(x * x, axis=-1, keepdims=True) + EPS) * g_ref[...]
    if last:
        (o_ref,) = rest
        o_ref[...] = normed
    else:
        next_mod_ref, x_out_ref, h_ref = rest
        nm = next_mod_ref[...]
        x_out_ref[...] = x
        h_ref[...] = (normed * (1.0 + nm[1:2]) + nm[0:1]).astype(h_ref.dtype)


def _out_proj(y, w_out, x2, mod3, g, next_mod3, seq):
    m, d = y.shape
    tm = 512
    per_seq = seq // tm
    last = next_mod3 is None
    row = pl.BlockSpec((tm, d), lambda i: (i, 0))
    mod_spec = pl.BlockSpec((None, 3, d), lambda i: (i // per_seq, 0, 0))
    in_specs = [row, pl.BlockSpec((d, d), lambda i: (0, 0)), row, mod_spec,
                pl.BlockSpec((1, d), lambda i: (0, 0))]
    args = [y, w_out, x2, mod3, g.reshape(1, d)]
    if last:
        out_specs = row
        out_shape = jax.ShapeDtypeStruct((m, d), F32)
    else:
        in_specs.append(mod_spec)
        args.append(next_mod3)
        out_specs = [row, row]
        out_shape = [jax.ShapeDtypeStruct((m, d), F32), jax.ShapeDtypeStruct((m, d), BF16)]
    return pl.pallas_call(
        functools.partial(_out_kernel, last=last),
        grid=(m // tm,),
        in_specs=in_specs,
        out_specs=out_specs,
        out_shape=out_shape,
        compiler_params=_params("parallel"),
        name="out_proj",
    )(*args)


def _prep_weights(w_in):
    depth, d, n_in = w_in.shape
    kvw = KV_GROUPS * HEAD_DIM
    wt = jnp.swapaxes(w_in, 1, 2)
    sizes = (d, d, d, d, 6 * kvw, 3 * N_HEADS, d, d, d, 3 * d)
    offs = np.concatenate([[0], np.cumsum(sizes)])
    a_end, q0, kv0, gsel0, tail0 = offs[3], offs[3], offs[4], offs[5], offs[6]
    gs = wt[:, gsel0:tail0].reshape(depth, 3, KV_GROUPS, HEADS_PER_GROUP, d).transpose(0, 2, 1, 3, 4)
    gs = gs.reshape(depth, KV_GROUPS, 3 * HEADS_PER_GROUP, d)
    gs = jnp.pad(gs, ((0, 0), (0, 0), (0, V7X_LANES - 3 * HEADS_PER_GROUP), (0, 0)))
    gs = gs.reshape(depth, KV_GROUPS * V7X_LANES, d)
    q_scale = HEAD_DIM ** -0.5 * LOG2_E
    w_all = jnp.concatenate([wt[:, q0:kv0] * q_scale, wt[:, tail0:], wt[:, kv0:gsel0], gs,
                             wt[:, :a_end]], axis=1)
    assert w_all.shape[1] == W_COLS
    return w_all.astype(BF16)


def kernel(x, c, rel_bias, norm_g, w_ada, b_ada, w_in, a_ln_g, a_ln_b, a_w_s, a_b_s, b_w_cmp1,
           b_w_cmp2, b_pos_cmp, c_w_grp, c_scale, w_branch, w_out, final_g):
    bsz, seq, d = x.shape
    depth = w_in.shape[0]
    assert d == D_MODEL and seq % 512 == 0
    x2 = x.reshape(bsz * seq, d)
    mod = _ada(c, w_ada, b_ada)
    bias_tile, bias_cmp = _bias_tables(rel_bias, seq)
    consts = _attention_constants(seq)
    mod3 = [mod[l].reshape(bsz, 3, d) for l in range(depth)]
    h = _prenorm(x2, norm_g[0], mod3[0], seq)
    w_all = _prep_weights(w_in)
    for l in range(depth):
        z = _proj(h, w_all, l, Z_COLS, BF16)
        ya = _proj_sgu(h, w_all, l, a_ln_g[l], a_ln_b[l], a_w_s[l], a_b_s[l])
        yc = _pool(z, c_w_grp[l], c_scale[l], seq)
        kvc = _compress(z, b_w_cmp1[l], b_w_cmp2[l], b_pos_cmp[l], bsz, seq)
        yb = _attention(z, kvc, bias_tile, bias_cmp, consts, bsz, seq)
        y = _merge(ya, yb, yc, z, w_branch[l].astype(BF16))
        if l + 1 < depth:
            x2, h = _out_proj(y, w_out[l].astype(BF16), x2, mod3[l], norm_g[l + 1], mod3[l + 1], seq)
        else:
            out = _out_proj(y, w_out[l].astype(BF16), x2, mod3[l], final_g, None, seq)
    return out.reshape(bsz, seq, d)
```

```python
import functools
import math

import numpy as np
import jax
import jax.numpy as jnp
from jax import lax
from jax.experimental import pallas as pl
from jax.experimental.pallas import tpu as pltpu

D_MODEL = 2048
A_GROUPS = 8
A_CHUNK = 128
HEAD_DIM = 128
N_HEADS = 16
KV_GROUPS = 4
HEADS_PER_GROUP = N_HEADS // KV_GROUPS
CMP_BLOCK = 32
CMP_STRIDE = 16
SEL_BLOCK = 64
N_SELECT = 16
WINDOW = 512
FORCE_BONUS = 1e4
POOL_WINDOWS = (2, 4, 8, 16)
C_GROUP = D_MODEL // len(POOL_WINDOWS)
REL_BUCKETS = 32
REL_MAX_DIST = 128
EPS = 1e-6
NEG_INF = -1e30

V7X_LANES = 128
V7X_VMEM_BYTES = 64 * 1024 * 1024
VMEM_LIMIT = 48 * 1024 * 1024
VMEM_LIMIT_BIG_TILES = 56 * 1024 * 1024

ATT_TILE = 128
KEY_TILE = 256
LOG2_E = math.log2(math.e)
POOL_HALO = 16
BF16 = jnp.bfloat16
F32 = jnp.float32

Z_Q = 0
Z_BG = 2048
Z_CX, Z_CG = 4096, 6144
Z_MERGE = 8192
Z_KV = 14336
Z_GSEL = 17408
Z_COLS = 17920
W_A = Z_COLS
W_COLS = W_A + 3 * D_MODEL
KV_KINDS = ("k_c", "v_c", "k_s", "v_s", "k_w", "v_w")


def _dot(a, b):
    return jnp.dot(a, b, preferred_element_type=F32)


def _dot_nt(a, b):
    return lax.dot_general(a, b, (((1,), (1,)), ((), ())), preferred_element_type=F32)


def _params(*sem, vmem=VMEM_LIMIT):
    return pltpu.CompilerParams(dimension_semantics=sem, vmem_limit_bytes=vmem)


def _ada_kernel(c_ref, w_ref, b_ref, o_ref):
    a = jax.nn.silu(c_ref[...]).astype(BF16)
    o_ref[...] = _dot(a, w_ref[...].astype(BF16)) + b_ref[...]


def _ada(c, w_ada, b_ada):
    depth, d, n = w_ada.shape
    bsz = c.shape[0]
    tn = 1024
    return pl.pallas_call(
        _ada_kernel,
        grid=(depth, n // tn),
        in_specs=[
            pl.BlockSpec((bsz, d), lambda l, j: (0, 0)),
            pl.BlockSpec((None, d, tn), lambda l, j: (l, 0, j)),
            pl.BlockSpec((None, 1, tn), lambda l, j: (l, 0, j)),
        ],
        out_specs=pl.BlockSpec((None, bsz, tn), lambda l, j: (l, 0, j)),
        out_shape=jax.ShapeDtypeStruct((depth, bsz, n), F32),
        compiler_params=_params("parallel", "parallel"),
        name="ada_mod",
    )(c, w_ada, b_ada.reshape(depth, 1, n))


def _prenorm_kernel(x_ref, g_ref, mod_ref, o_ref):
    x = x_ref[...]
    y = x * lax.rsqrt(jnp.mean(x * x, axis=-1, keepdims=True) + EPS) * g_ref[...]
    m = mod_ref[...]
    o_ref[...] = (y * (1.0 + m[1:2]) + m[0:1]).astype(o_ref.dtype)


def _prenorm(x2, g, mod3, seq):
    m, d = x2.shape
    tm = 512
    per_seq = seq // tm
    return pl.pallas_call(
        _prenorm_kernel,
        grid=(m // tm,),
        in_specs=[
            pl.BlockSpec((tm, d), lambda i: (i, 0)),
            pl.BlockSpec((1, d), lambda i: (0, 0)),
            pl.BlockSpec((None, 3, d), lambda i: (i // per_seq, 0, 0)),
        ],
        out_specs=pl.BlockSpec((tm, d), lambda i: (i, 0)),
        out_shape=jax.ShapeDtypeStruct((m, d), BF16),
        compiler_params=_params("parallel"),
        name="prenorm",
    )(x2, g.reshape(1, d), mod3)


def _proj_kernel(a_ref, w_ref, o_ref):
    o_ref[...] = _dot_nt(a_ref[...], w_ref[...]).astype(o_ref.dtype)


def _proj(a, w_all, layer, n, out_dtype, tm=1024, tn=1280):
    m, k = a.shape
    tm = min(tm, m)
    return pl.pallas_call(
        _proj_kernel,
        grid=(m // tm, n // tn),
        in_specs=[
            pl.BlockSpec((tm, k), lambda i, j: (i, 0)),
            pl.BlockSpec((None, tn, k), lambda i, j: (layer, j, 0)),
        ],
        out_specs=pl.BlockSpec((tm, tn), lambda i, j: (i, j)),
        out_shape=jax.ShapeDtypeStruct((m, n), out_dtype),
        compiler_params=_params("parallel", "arbitrary"),
        name="in_proj",
    )(a, w_all)


def _proj_sgu_kernel(h_ref, wu_ref, wv_ref, wg_ref, lng_ref, lnb_ref, ws_ref, bst_ref, o_ref,
                     v_buf, p_buf):
    j = pl.program_id(1)
    n_j, tm, tn = v_buf.shape
    cg = D_MODEL // A_GROUPS
    h = h_ref[...]
    v_buf[j] = jax.nn.gelu(_dot_nt(h, wv_ref[...]))
    p_buf[j] = (jax.nn.gelu(_dot_nt(h, wu_ref[...]))
                * jax.nn.silu(_dot_nt(h, wg_ref[...]))).astype(p_buf.dtype)

    @pl.when(j == n_j - 1)
    def _():
        total = jnp.sum(v_buf[0], axis=-1, keepdims=True)
        for jj in range(1, n_j):
            total = total + jnp.sum(v_buf[jj], axis=-1, keepdims=True)
        mu = total * (1.0 / D_MODEL)
        sq = jnp.zeros_like(mu)
        for jj in range(n_j):
            dv = v_buf[jj] - mu
            sq = sq + jnp.sum(dv * dv, axis=-1, keepdims=True)
        rstd = lax.rsqrt(sq * (1.0 / D_MODEL) + EPS)
        row = lax.broadcasted_iota(jnp.int32, (A_CHUNK, A_CHUNK), 0)
        col = lax.broadcasted_iota(jnp.int32, (A_CHUNK, A_CHUNK), 1)
        tril = row >= col
        bst = bst_ref[...]
        for g in range(A_GROUPS):
            ws = jnp.where(tril, ws_ref[g], 0.0).astype(BF16)
            jj, off = divmod(g * cg, tn)
            cs = slice(g * cg, (g + 1) * cg)
            for c in range(tm // A_CHUNK):
                rs = slice(c * A_CHUNK, (c + 1) * A_CHUNK)
                vn = ((v_buf[jj, rs, off:off + cg] - mu[rs]) * rstd[rs] * lng_ref[:, cs]
                      + lnb_ref[:, cs]).astype(BF16)
                mixed = _dot(ws, vn) + bst[:, g:g + 1]
                o_ref[rs, cs] = (p_buf[jj, rs, off:off + cg] * mixed).astype(o_ref.dtype)


def _proj_sgu(h, w_all, layer, ln_g, ln_b, w_s, b_s):
    m, d = h.shape
    tm, tn = min(1024, m), 512
    per = d // tn
    a0 = W_A // tn
    return pl.pallas_call(
        _proj_sgu_kernel,
        grid=(m // tm, per),
        in_specs=[
            pl.BlockSpec((tm, d), lambda i, j: (i, 0)),
            pl.BlockSpec((None, tn, d), lambda i, j: (layer, a0 + j, 0)),
            pl.BlockSpec((None, tn, d), lambda i, j: (layer, a0 + per + j, 0)),
            pl.BlockSpec((None, tn, d), lambda i, j: (layer, a0 + 2 * per + j, 0)),
            pl.BlockSpec((1, d), lambda i, j: (0, 0)),
            pl.BlockSpec((1, d), lambda i, j: (0, 0)),
            pl.BlockSpec((A_GROUPS, A_CHUNK, A_CHUNK), lambda i, j: (0, 0, 0)),
            pl.BlockSpec((A_CHUNK, A_GROUPS), lambda i, j: (0, 0)),
        ],
        out_specs=pl.BlockSpec((tm, d), lambda i, j: (i, 0)),
        out_shape=jax.ShapeDtypeStruct((m, d), BF16),
        scratch_shapes=[pltpu.VMEM((per, tm, tn), F32), pltpu.VMEM((per, tm, tn), BF16)],
        compiler_params=_params("parallel", "arbitrary", vmem=VMEM_LIMIT_BIG_TILES),
        name="proj_sgu",
    )(h, w_all, w_all, w_all, ln_g.reshape(1, d), ln_b.reshape(1, d), w_s, b_s.T)


def _pool_kernel(x_ref, halo_ref, gate_ref, wg_ref, ls_ref, o_ref, ext_ref, *, tiles_per_seq):
    tm = x_ref.shape[0]
    i = pl.program_id(0)
    pos0 = (i % tiles_per_seq) * tm
    halo = halo_ref[...].astype(F32)
    ext_ref[0:POOL_HALO, :] = jnp.where(pos0 == 0, jnp.zeros_like(halo), halo)
    ext_ref[POOL_HALO:POOL_HALO + tm, :] = x_ref[...].astype(F32)
    tpos = pos0 + lax.broadcasted_iota(jnp.int32, (tm, 1), 0)
    for gi, w in enumerate(POOL_WINDOWS):
        cs = slice(gi * C_GROUP, (gi + 1) * C_GROUP)
        x = ext_ref[POOL_HALO:POOL_HALO + tm, cs]
        s = x
        for back in range(1, w):
            s = s + ext_ref[POOL_HALO - back:POOL_HALO - back + tm, cs]
        cnt = jnp.minimum(tpos + 1, w).astype(F32)
        y = (s / cnt - x).astype(BF16)
        yg = (_dot(y, wg_ref[gi].astype(BF16)) * ls_ref[:, cs]
              * jax.nn.silu(gate_ref[:, cs].astype(F32)))
        o_ref[:, cs] = yg.astype(o_ref.dtype)


def _pool(zf, w_grp, ls, seq):
    m = zf.shape[0]
    tm = 256
    d = D_MODEL
    halo_blocks = tm // POOL_HALO
    return pl.pallas_call(
        functools.partial(_pool_kernel, tiles_per_seq=seq // tm),
        grid=(m // tm,),
        in_specs=[
            pl.BlockSpec((tm, d), lambda i: (i, Z_CX // d)),
            pl.BlockSpec((POOL_HALO, d), lambda i: (jnp.maximum(i * halo_blocks - 1, 0), Z_CX // d)),
            pl.BlockSpec((tm, d), lambda i: (i, Z_CG // d)),
            pl.BlockSpec((len(POOL_WINDOWS), C_GROUP, C_GROUP), lambda i: (0, 0, 0)),
            pl.BlockSpec((1, d), lambda i: (0, 0)),
        ],
        out_specs=pl.BlockSpec((tm, d), lambda i: (i, 0)),
        out_shape=jax.ShapeDtypeStruct((m, d), BF16),
        scratch_shapes=[pltpu.VMEM((POOL_HALO + tm, d), F32)],
        compiler_params=_params("parallel"),
        name="pool",
    )(zf, zf, zf, w_grp, ls.reshape(1, d))


def _compress_kernel(x_ref, w1_ref, w2_ref, pos_ref, o_ref, xf_ref):
    n_chunks = x_ref.shape[0] // CMP_STRIDE
    xf_ref[...] = x_ref[...].astype(F32)
    pos = pos_ref[...]
    first = jnp.zeros((n_chunks, HEAD_DIM), F32)
    second = jnp.zeros((n_chunks, HEAD_DIM), F32)
    for l in range(CMP_STRIDE):
        xl = xf_ref[pl.ds(l, n_chunks, stride=CMP_STRIDE), :]
        first += _dot((xl + pos[l:l + 1]).astype(BF16), w1_ref[l].astype(BF16))
        second += _dot((xl + pos[CMP_STRIDE + l:CMP_STRIDE + l + 1]).astype(BF16),
                       w1_ref[CMP_STRIDE + l].astype(BF16))
    hdn = jax.nn.gelu(first + pltpu.roll(second, n_chunks - 1, axis=0))
    o_ref[...] = _dot(hdn.astype(BF16), w2_ref[...].astype(BF16)).astype(o_ref.dtype)


def _compress(z, w1, w2, pos, bsz, seq):
    n_chunks = seq // CMP_STRIDE
    kv0 = Z_KV // HEAD_DIM
    return pl.pallas_call(
        _compress_kernel,
        grid=(bsz, 2, KV_GROUPS),
        in_specs=[
            pl.BlockSpec((seq, HEAD_DIM), lambda b, t, g: (b, kv0 + t * KV_GROUPS + g)),
            pl.BlockSpec((None, CMP_BLOCK, HEAD_DIM, HEAD_DIM), lambda b, t, g: (t, 0, 0, 0)),
            pl.BlockSpec((None, HEAD_DIM, HEAD_DIM), lambda b, t, g: (t, 0, 0)),
            pl.BlockSpec((None, CMP_BLOCK, HEAD_DIM), lambda b, t, g: (t, 0, 0)),
        ],
        out_specs=pl.BlockSpec((None, None, None, n_chunks, HEAD_DIM), lambda b, t, g: (b, t, g, 0, 0)),
        out_shape=jax.ShapeDtypeStruct((bsz, 2, KV_GROUPS, n_chunks, HEAD_DIM), BF16),
        scratch_shapes=[pltpu.VMEM((seq, HEAD_DIM), F32)],
        compiler_params=_params("parallel", "parallel", "parallel"),
        name="compress",
    )(z, w1, w2, pos)


def _bucket_of_distance(n):
    n = np.asarray(n, np.int64)
    max_exact = REL_BUCKETS // 2
    nf = np.maximum(n, 1).astype(np.float64)
    large = max_exact + (np.log(nf / max_exact) / math.log(REL_MAX_DIST / max_exact)
                         * (REL_BUCKETS - max_exact)).astype(np.int64)
    large = np.minimum(large, REL_BUCKETS - 1)
    return np.where(n < max_exact, n, large).astype(np.int32)


def _bias_kernel(tab_ref, idx_tile_ref, idx_cmp_ref, tile_ref, cmp_ref):
    h = pl.program_id(0) * HEADS_PER_GROUP + pl.program_id(1)
    far = tab_ref[REL_BUCKETS - 1, h]
    for idx_ref, out_ref, shift in ((idx_tile_ref, tile_ref, far), (idx_cmp_ref, cmp_ref, 0.0)):
        idx = idx_ref[...]
        acc = jnp.full(idx.shape, NEG_INF, F32)
        for b in range(REL_BUCKETS):
            acc = jnp.where(idx == b, (tab_ref[b, h] - shift) * LOG2_E, acc)
        out_ref[...] = acc


N_BIAS_TILES = 7
MASKED_TILE = N_BIAS_TILES - 1


def _bias_tables(rel_bias, seq):
    t = ATT_TILE
    kt = KEY_TILE
    n_q = seq // t
    r = np.arange(t)[None, :]
    masked = REL_BUCKETS

    def tile(offset, windowed):
        d = offset + r - np.arange(kt)[:, None]
        ok = (d >= 0) & ((d < WINDOW) | (not windowed))
        return np.where(ok, _bucket_of_distance(np.maximum(d, 0)), masked)

    idx_tile = np.stack([tile(0, False), tile(t, False), tile(2 * t, False), tile(3 * t, False),
                         tile(4 * t, True), tile(5 * t, True),
                         np.full((kt, t), masked)]).astype(np.int32)
    assert idx_tile.shape[0] == N_BIAS_TILES and (idx_tile[3] == REL_BUCKETS - 1).all()
    assert (tile(3 * t, True) == idx_tile[3]).all()
    cmp_end = np.arange(t)[:, None] * CMP_STRIDE + CMP_BLOCK - 1
    idx_cmp = np.stack([_bucket_of_distance(np.maximum(i * t + r - cmp_end, 0))
                        for i in range(n_q)]).astype(np.int32)
    return pl.pallas_call(
        _bias_kernel,
        grid=(KV_GROUPS, HEADS_PER_GROUP),
        in_specs=[
            pl.BlockSpec(memory_space=pltpu.SMEM),
            pl.BlockSpec((N_BIAS_TILES, kt, t), lambda g, j: (0, 0, 0)),
            pl.BlockSpec((n_q, t, t), lambda g, j: (0, 0, 0)),
        ],
        out_specs=[
            pl.BlockSpec((None, N_BIAS_TILES, kt, t), lambda g, j: (g, 0, 0, j)),
            pl.BlockSpec((None, n_q, t, t), lambda g, j: (g, 0, 0, j)),
        ],
        out_shape=[
            jax.ShapeDtypeStruct((KV_GROUPS, N_BIAS_TILES, kt, HEADS_PER_GROUP * t), F32),
            jax.ShapeDtypeStruct((KV_GROUPS, n_q, t, HEADS_PER_GROUP * t), F32),
        ],
        compiler_params=_params("parallel", "parallel"),
        name="bias_tables",
    )(rel_bias, jnp.asarray(idx_tile), jnp.asarray(idx_cmp))


def _attn_kernel(q_ref, ks_ref, vs_ref, kw_ref, vw_ref, kc_ref, vc_ref, gsel_ref, bgate_ref,
                 btile_ref, bcmp_ref, ov_ref, expand_ref, eye_ref, o_ref,
                 vst_ref, vwt_ref, kaug_ref, s_ref, w_ref, osel_ref, owin_ref):
    t = ATT_TILE
    kt = KEY_TILE
    hg = HEADS_PER_GROUP
    cols = hg * t
    n_kt = s_ref.shape[0]
    n_q = n_kt * kt // t
    i = pl.program_id(1)
    last_tile = lax.shift_right_logical(i, 1)
    eye = eye_ref[...]

    @pl.when(i == 0)
    def _():
        kaug_ref[:, 0:HEAD_DIM] = ks_ref[...]
        kaug_ref[:, HEAD_DIM:] = expand_ref[...]
        for jt in range(n_kt):
            ks = slice(jt * kt, (jt + 1) * kt)
            vst_ref[jt] = _dot_nt(eye, vs_ref[ks, :]).astype(BF16)
            vwt_ref[jt] = _dot_nt(eye, vw_ref[ks, :]).astype(BF16)

    qb = q_ref[...]
    qs = jnp.concatenate([qb[:, j * HEAD_DIM:(j + 1) * HEAD_DIM] for j in range(hg)], axis=0)

    def bias_index(delta):
        return jnp.where(delta < 0, MASKED_TILE, delta)

    win_tiles = WINDOW // kt + 1
    win_first = jnp.maximum(last_tile - (win_tiles - 1), 0)

    def win_score_tile(js):
        jt = win_first + js
        k = kw_ref[pl.ds(pl.multiple_of(jt * kt, kt), kt), :]
        st = _dot_nt(k, qs) + btile_ref[bias_index(i - 2 * jt)]
        w_ref[js] = st
        return st

    n_idx = lax.broadcasted_iota(jnp.int32, (t, cols), 0)
    tq = i * t + (lax.broadcasted_iota(jnp.int32, (t, cols), 1) & (t - 1))
    valid = tq >= n_idx * CMP_STRIDE + (CMP_BLOCK - 1)
    s = jnp.where(valid, _dot_nt(kc_ref[...], qs) + bcmp_ref[...], NEG_INF)
    win_mx = win_score_tile(0)
    e = jnp.where(valid, jnp.exp2(s - jnp.max(s, axis=0, keepdims=True)), 0.0)
    l = jnp.sum(e, axis=0, keepdims=True)
    p_c = e * (1.0 / jnp.where(l > 0.0, l, 1.0))
    vct = _dot_nt(eye, vc_ref[...]).astype(BF16)
    ocmp_t = _dot(vct, p_c.astype(BF16))

    psum = p_c[:, 0:t]
    for j in range(1, hg):
        psum = psum + p_c[:, j * t:(j + 1) * t]
    p_hi = psum.astype(BF16)
    p_lo = (psum - p_hi.astype(F32)).astype(BF16)
    ov = ov_ref[...]
    n_sel = n_kt * kt // SEL_BLOCK
    imp = (_dot(ov, p_hi) + _dot(ov, p_lo))[0:n_sel]
    win_mx = jnp.maximum(win_mx, win_score_tile(1))
    blk = lax.broadcasted_iota(jnp.int32, (n_sel, t), 0)
    tqb = i * t + lax.broadcasted_iota(jnp.int32, (n_sel, t), 1)
    cur = lax.shift_right_logical(tqb, int(math.log2(SEL_BLOCK)))
    forced = (blk == 0) | (blk == cur) | (blk == cur - 1)
    future = blk * SEL_BLOCK > tqb
    imp = jnp.where(future, -1.0, imp + jnp.where(forced, FORCE_BONUS, 0.0))
    beaten_by = []
    for sp in range(n_sel):
        other = imp[sp:sp + 1, :]
        beats = (other > imp) | ((other == imp) & (blk > sp))
        beaten_by.append(jnp.where(beats, 1.0, 0.0))
    while len(beaten_by) > 1:
        beaten_by = [a + b for a, b in zip(beaten_by[::2], beaten_by[1::2])]
    rank = beaten_by[0]
    unsel = jnp.where(rank < float(min(N_SELECT, n_sel)), 0.0, 1.0)
    unsel_pad = jnp.concatenate([unsel, jnp.zeros((t - n_sel, t), F32)], axis=0).astype(BF16)
    unsel_qb = _dot_nt(eye, unsel_pad).astype(BF16)
    for js in range(2, win_tiles):
        win_mx = jnp.maximum(win_mx, win_score_tile(js))
    m_win = jnp.max(win_mx, axis=0, keepdims=True)
    q_aug = jnp.concatenate([qs, jnp.concatenate([unsel_qb] * hg, axis=0)], axis=1)

    def value_tile(carry, m, vt, sc_ref, js):
        lsum, acc = carry
        p = jnp.exp2(sc_ref[js] - m)
        return lsum + p, acc + _dot(vt, p.astype(BF16))

    def normalised(carry):
        lsum, acc = carry
        return acc * (1.0 / jnp.sum(lsum, axis=0, keepdims=True))

    zero_carry = (jnp.zeros((kt, cols), F32), jnp.zeros((HEAD_DIM, cols), F32))

    q_per_variant = kt // t
    far_offset = 3
    for variant in range(n_q // q_per_variant):
        n_sub = variant + 1
        n_far = max((q_per_variant * variant - far_offset) // 2 + 1, 0)
        first = max(variant - (win_tiles - 1), 0)

        def sel_bias(js, n_far=n_far):
            if js < n_far:
                return None
            return btile_ref[bias_index(jnp.minimum(i - 2 * js, far_offset))]

        @pl.when(last_tile == variant)
        def _(n_sub=n_sub, sel_bias=sel_bias, first=first):
            mx = jnp.full((kt, cols), NEG_INF, F32)
            for js in range(n_sub):
                st = _dot_nt(kaug_ref[js * kt:(js + 1) * kt, :], q_aug)
                bias = sel_bias(js)
                if bias is not None:
                    st = st + bias
                s_ref[js] = st
                mx = jnp.maximum(mx, st)
            win = zero_carry
            for js in range(win_tiles):
                win = value_tile(win, m_win, vwt_ref[first + js], w_ref, js)
            owin_ref[...] = normalised(win)
            m_sel = jnp.max(mx, axis=0, keepdims=True)
            sel = zero_carry
            for js in range(n_sub):
                sel = value_tile(sel, m_sel, vst_ref[js], s_ref, js)
            osel_ref[...] = normalised(sel)

    osel_t = osel_ref[...]
    owin_t = owin_ref[...]

    gt = jnp.transpose(jax.nn.sigmoid(gsel_ref[...].astype(F32)))
    for j in range(hg):
        cs = slice(j * t, (j + 1) * t)
        ot = (gt[j:j + 1] * ocmp_t[:, cs] + gt[hg + j:hg + j + 1] * osel_t[:, cs]
              + gt[2 * hg + j:2 * hg + j + 1] * owin_t[:, cs])
        hs = slice(j * HEAD_DIM, (j + 1) * HEAD_DIM)
        o_ref[:, hs] = (jnp.transpose(ot)
                        * jax.nn.silu(bgate_ref[:, hs].astype(F32))).astype(o_ref.dtype)


def _attention_constants(seq):
    t = ATT_TILE
    n_cmp_pad = seq // CMP_STRIDE
    assert n_cmp_pad == t, "one tile of compressed keys per sequence"
    n_cmp = n_cmp_pad - CMP_BLOCK // CMP_STRIDE + 1
    n_sel = seq // SEL_BLOCK
    cst = np.arange(n_cmp_pad) * CMP_STRIDE
    sst = np.arange(n_sel) * SEL_BLOCK
    overlap = ((cst[:, None] < sst[None] + SEL_BLOCK) & (cst[:, None] + CMP_BLOCK > sst[None]))
    overlap[n_cmp:] = False
    ov = np.zeros((t, n_cmp_pad), np.float32)
    ov[:n_sel] = overlap.T
    keys = np.arange(seq)
    expand = np.zeros((seq, t), np.float32)
    expand[keys, keys // SEL_BLOCK] = NEG_INF
    eye = np.eye(t, dtype=np.float32)
    return (jnp.asarray(ov, BF16), jnp.asarray(expand, BF16), jnp.asarray(eye, BF16))


def _attention(z, kvc, bias_tile, bias_cmp, consts, bsz, seq):
    t = ATT_TILE
    kt = KEY_TILE
    n_q = seq // t
    n_kt = seq // kt
    gw = HEADS_PER_GROUP * HEAD_DIM
    kv0 = Z_KV // HEAD_DIM
    ov, expand, eye = consts

    def kv_spec(kind):
        return pl.BlockSpec((seq, HEAD_DIM),
                            lambda bg, i: (bg % bsz, kv0 + kind * KV_GROUPS + bg // bsz))

    def cmp_spec(kind):
        return pl.BlockSpec((None, None, None, t, HEAD_DIM),
                            lambda bg, i: (bg % bsz, kind, bg // bsz, 0, 0))

    def tok(bg, i):
        return (bg % bsz) * n_q + i

    return pl.pallas_call(
        _attn_kernel,
        grid=(bsz * KV_GROUPS, n_q),
        in_specs=[
            pl.BlockSpec((t, gw), lambda bg, i: (tok(bg, i), Z_Q // gw + bg // bsz)),
            kv_spec(KV_KINDS.index("k_s")), kv_spec(KV_KINDS.index("v_s")),
            kv_spec(KV_KINDS.index("k_w")), kv_spec(KV_KINDS.index("v_w")),
            cmp_spec(0), cmp_spec(1),
            pl.BlockSpec((t, V7X_LANES), lambda bg, i: (tok(bg, i), Z_GSEL // V7X_LANES + bg // bsz)),
            pl.BlockSpec((t, gw), lambda bg, i: (tok(bg, i), Z_BG // gw + bg // bsz)),
            pl.BlockSpec((None, N_BIAS_TILES, kt, gw), lambda bg, i: (bg // bsz, 0, 0, 0)),
            pl.BlockSpec((None, None, t, gw), lambda bg, i: (bg // bsz, i, 0, 0)),
            pl.BlockSpec((t, t), lambda bg, i: (0, 0)),
            pl.BlockSpec((seq, t), lambda bg, i: (0, 0)),
            pl.BlockSpec((t, t), lambda bg, i: (0, 0)),
        ],
        out_specs=pl.BlockSpec((t, gw), lambda bg, i: (tok(bg, i), bg // bsz)),
        out_shape=jax.ShapeDtypeStruct((bsz * seq, D_MODEL), BF16),
        scratch_shapes=[pltpu.VMEM((n_kt, HEAD_DIM, kt), BF16),
                        pltpu.VMEM((n_kt, HEAD_DIM, kt), BF16),
                        pltpu.VMEM((seq, HEAD_DIM + t), BF16),
                        pltpu.VMEM((n_kt, kt, gw), F32),
                        pltpu.VMEM((WINDOW // kt + 1, kt, gw), F32),
                        pltpu.VMEM((HEAD_DIM, gw), F32),
                        pltpu.VMEM((HEAD_DIM, gw), F32)],
        compiler_params=_params("parallel", "arbitrary"),
        name="nsa_attention",
    )(z, z, z, z, z, kvc, kvc, z, z, bias_tile, bias_cmp, ov, expand, eye)


def _merge_kernel(ya_ref, yb_ref, yc_ref, ma_ref, mb_ref, mc_ref, w_ref, o_ref):
    y = jax.nn.sigmoid(ma_ref[...].astype(F32)) * _dot(ya_ref[...], w_ref[0])
    y += jax.nn.sigmoid(mb_ref[...].astype(F32)) * _dot(yb_ref[...], w_ref[1])
    y += jax.nn.sigmoid(mc_ref[...].astype(F32)) * _dot(yc_ref[...], w_ref[2])
    o_ref[...] = y.astype(o_ref.dtype)


def _merge(ya, yb, yc, zf, w_branch):
    m, d = ya.shape
    tm, tn = min(1024, m), 512
    m0 = Z_MERGE // tn
    per = d // tn
    y_spec = pl.BlockSpec((tm, d), lambda i, j: (i, 0))
    return pl.pallas_call(
        _merge_kernel,
        grid=(m // tm, d // tn),
        in_specs=[
            y_spec, y_spec, y_spec,
            pl.BlockSpec((tm, tn), lambda i, j: (i, m0 + j)),
            pl.BlockSpec((tm, tn), lambda i, j: (i, m0 + per + j)),
            pl.BlockSpec((tm, tn), lambda i, j: (i, m0 + 2 * per + j)),
            pl.BlockSpec((3, d, tn), lambda i, j: (0, 0, j)),
        ],
        out_specs=pl.BlockSpec((tm, tn), lambda i, j: (i, j)),
        out_shape=jax.ShapeDtypeStruct((m, d), BF16),
        compiler_params=_params("parallel", "arbitrary", vmem=VMEM_LIMIT_BIG_TILES),
        name="merge",
    )(ya, yb, yc, zf, zf, zf, w_branch)


def _out_kernel(y_ref, w_ref, x_ref, mod_ref, g_ref, *rest, last):
    x = x_ref[...] + mod_ref[2:3, :] * _dot(y_ref[...], w_ref[...])
    normed = x * lax.rsqrt(jnp.mean(x * x, axis=-1, keepdims=True) + EPS) * g_ref[...]
    if last:
        (o_ref,) = rest
        o_ref[...] = normed
    else:
        next_mod_ref, x_out_ref, h_ref = rest
        nm = next_mod_ref[...]
        x_out_ref[...] = x
        h_ref[...] = (normed * (1.0 + nm[1:2]) + nm[0:1]).astype(h_ref.dtype)


def _out_proj(y, w_out, x2, mod3, g, next_mod3, seq):
    m, d = y.shape
    tm = 512
    per_seq = seq // tm
    last = next_mod3 is None
    row = pl.BlockSpec((tm, d), lambda i: (i, 0))
    mod_spec = pl.BlockSpec((None, 3, d), lambda i: (i // per_seq, 0, 0))
    in_specs = [row, pl.BlockSpec((d, d), lambda i: (0, 0)), row, mod_spec,
                pl.BlockSpec((1, d), lambda i: (0, 0))]
    args = [y, w_out, x2, mod3, g.reshape(1, d)]
    if last:
        out_specs = row
        out_shape = jax.ShapeDtypeStruct((m, d), F32)
    else:
        in_specs.append(mod_spec)
        args.append(next_mod3)
        out_specs = [row, row]
        out_shape = [jax.ShapeDtypeStruct((m, d), F32), jax.ShapeDtypeStruct((m, d), BF16)]
    return pl.pallas_call(
        functools.partial(_out_kernel, last=last),
        grid=(m // tm,),
        in_specs=in_specs,
        out_specs=out_specs,
        out_shape=out_shape,
        compiler_params=_params("parallel"),
        name="out_proj",
    )(*args)


def _prep_weights(w_in):
    depth, d, n_in = w_in.shape
    kvw = KV_GROUPS * HEAD_DIM
    wt = jnp.swapaxes(w_in, 1, 2)
    sizes = (d, d, d, d, 6 * kvw, 3 * N_HEADS, d, d, d, 3 * d)
    offs = np.concatenate([[0], np.cumsum(sizes)])
    a_end, q0, kv0, gsel0, tail0 = offs[3], offs[3], offs[4], offs[5], offs[6]
    gs = wt[:, gsel0:tail0].reshape(depth, 3, KV_GROUPS, HEADS_PER_GROUP, d).transpose(0, 2, 1, 3, 4)
    gs = gs.reshape(depth, KV_GROUPS, 3 * HEADS_PER_GROUP, d)
    gs = jnp.pad(gs, ((0, 0), (0, 0), (0, V7X_LANES - 3 * HEADS_PER_GROUP), (0, 0)))
    gs = gs.reshape(depth, KV_GROUPS * V7X_LANES, d)
    q_scale = HEAD_DIM ** -0.5 * LOG2_E
    w_all = jnp.concatenate([wt[:, q0:kv0] * q_scale, wt[:, tail0:], wt[:, kv0:gsel0], gs,
                             wt[:, :a_end]], axis=1)
    assert w_all.shape[1] == W_COLS
    return w_all.astype(BF16)


def kernel(x, c, rel_bias, norm_g, w_ada, b_ada, w_in, a_ln_g, a_ln_b, a_w_s, a_b_s, b_w_cmp1,
           b_w_cmp2, b_pos_cmp, c_w_grp, c_scale, w_branch, w_out, final_g):
    bsz, seq, d = x.shape
    depth = w_in.shape[0]
    assert d == D_MODEL and seq % 512 == 0
    x2 = x.reshape(bsz * seq, d)
    mod = _ada(c, w_ada, b_ada)
    bias_tile, bias_cmp = _bias_tables(rel_bias, seq)
    consts = _attention_constants(seq)
    mod3 = [mod[l].reshape(bsz, 3, d) for l in range(depth)]
    h = _prenorm(x2, norm_g[0], mod3[0], seq)
    w_all = _prep_weights(w_in)
    for l in range(depth):
        z = _proj(h, w_all, l, Z_COLS, BF16)
        ya = _proj_sgu(h, w_all, l, a_ln_g[l], a_ln_b[l], a_w_s[l], a_b_s[l])
        yc = _pool(z, c_w_grp[l], c_scale[l], seq)
        kvc = _compress(z, b_w_cmp1[l], b_w_cmp2[l], b_pos_cmp[l], bsz, seq)
        yb = _attention(z, kvc, bias_tile, bias_cmp, consts, bsz, seq)
        y = _merge(ya, yb, yc, z, w_branch[l].astype(BF16))
        if l + 1 < depth:
            x2, h = _out_proj(y, w_out[l].astype(BF16), x2, mod3[l], norm_g[l + 1], mod3[l + 1], seq)
        else:
            out = _out_proj(y, w_out[l].astype(BF16), x2, mod3[l], final_g, None, seq)
    return out.reshape(bsz, seq, d)
```

```python
import functools
import math

import numpy as np
import jax
import jax.numpy as jnp
from jax import lax
from jax.experimental import pallas as pl
from jax.experimental.pallas import tpu as pltpu

D_MODEL = 2048
A_GROUPS = 8
A_CHUNK = 128
HEAD_DIM = 128
N_HEADS = 16
KV_GROUPS = 4
HEADS_PER_GROUP = N_HEADS // KV_GROUPS
CMP_BLOCK = 32
CMP_STRIDE = 16
SEL_BLOCK = 64
N_SELECT = 16
WINDOW = 512
FORCE_BONUS = 1e4
POOL_WINDOWS = (2, 4, 8, 16)
C_GROUP = D_MODEL // len(POOL_WINDOWS)
REL_BUCKETS = 32
REL_MAX_DIST = 128
EPS = 1e-6
NEG_INF = -1e30

V7X_LANES = 128
V7X_VMEM_BYTES = 64 * 1024 * 1024
VMEM_LIMIT = 48 * 1024 * 1024
VMEM_LIMIT_BIG_TILES = 56 * 1024 * 1024

ATT_TILE = 128
KEY_TILE = 256
LOG2_E = math.log2(math.e)
POOL_HALO = 16
BF16 = jnp.bfloat16
F32 = jnp.float32

Z_Q = 0
Z_BG = 2048
Z_CX, Z_CG = 4096, 6144
Z_MERGE = 8192
Z_KV = 14336
Z_GSEL = 17408
Z_COLS = 17920
W_A = Z_COLS
W_COLS = W_A + 3 * D_MODEL
KV_KINDS = ("k_c", "v_c", "k_s", "v_s", "k_w", "v_w")


def _dot(a, b):
    return jnp.dot(a, b, preferred_element_type=F32)


def _dot_nt(a, b):
    return lax.dot_general(a, b, (((1,), (1,)), ((), ())), preferred_element_type=F32)


def _params(*sem, vmem=VMEM_LIMIT):
    return pltpu.CompilerParams(dimension_semantics=sem, vmem_limit_bytes=vmem)


def _ada_kernel(c_ref, w_ref, b_ref, o_ref):
    a = jax.nn.silu(c_ref[...]).astype(BF16)
    o_ref[...] = _dot(a, w_ref[...].astype(BF16)) + b_ref[...]


def _ada(c, w_ada, b_ada):
    depth, d, n = w_ada.shape
    bsz = c.shape[0]
    tn = 1024
    return pl.pallas_call(
        _ada_kernel,
        grid=(depth, n // tn),
        in_specs=[
            pl.BlockSpec((bsz, d), lambda l, j: (0, 0)),
            pl.BlockSpec((None, d, tn), lambda l, j: (l, 0, j)),
            pl.BlockSpec((None, 1, tn), lambda l, j: (l, 0, j)),
        ],
        out_specs=pl.BlockSpec((None, bsz, tn), lambda l, j: (l, 0, j)),
        out_shape=jax.ShapeDtypeStruct((depth, bsz, n), F32),
        compiler_params=_params("parallel", "parallel"),
        name="ada_mod",
    )(c, w_ada, b_ada.reshape(depth, 1, n))


def _prenorm_kernel(x_ref, g_ref, mod_ref, o_ref):
    x = x_ref[...]
    y = x * lax.rsqrt(jnp.mean(x * x, axis=-1, keepdims=True) + EPS) * g_ref[...]
    m = mod_ref[...]
    o_ref[...] = (y * (1.0 + m[1:2]) + m[0:1]).astype(o_ref.dtype)


def _prenorm(x2, g, mod3, seq):
    m, d = x2.shape
    tm = 512
    per_seq = seq // tm
    return pl.pallas_call(
        _prenorm_kernel,
        grid=(m // tm,),
        in_specs=[
            pl.BlockSpec((tm, d), lambda i: (i, 0)),
            pl.BlockSpec((1, d), lambda i: (0, 0)),
            pl.BlockSpec((None, 3, d), lambda i: (i // per_seq, 0, 0)),
        ],
        out_specs=pl.BlockSpec((tm, d), lambda i: (i, 0)),
        out_shape=jax.ShapeDtypeStruct((m, d), BF16),
        compiler_params=_params("parallel"),
        name="prenorm",
    )(x2, g.reshape(1, d), mod3)


def _proj_kernel(a_ref, w_ref, o_ref):
    o_ref[...] = _dot_nt(a_ref[...], w_ref[...]).astype(o_ref.dtype)


def _proj(a, w_all, layer, n, out_dtype, tm=1024, tn=2560):
    m, k = a.shape
    tm = min(tm, m)
    return pl.pallas_call(
        _proj_kernel,
        grid=(m // tm, n // tn),
        in_specs=[
            pl.BlockSpec((tm, k), lambda i, j: (i, 0)),
            pl.BlockSpec((None, tn, k), lambda i, j: (layer, j, 0)),
        ],
        out_specs=pl.BlockSpec((tm, tn), lambda i, j: (i, j)),
        out_shape=jax.ShapeDtypeStruct((m, n), out_dtype),
        compiler_params=_params("parallel", "arbitrary"),
        name="in_proj",
    )(a, w_all)


def _proj_sgu_kernel(h_ref, wu_ref, wv_ref, wg_ref, lng_ref, lnb_ref, ws_ref, bst_ref, o_ref,
                     v_buf, p_buf):
    j = pl.program_id(1)
    n_j, tm, tn = v_buf.shape
    cg = D_MODEL // A_GROUPS
    h = h_ref[...]
    v_buf[j] = jax.nn.gelu(_dot_nt(h, wv_ref[...]))
    p_buf[j] = (jax.nn.gelu(_dot_nt(h, wu_ref[...]))
                * jax.nn.silu(_dot_nt(h, wg_ref[...]))).astype(p_buf.dtype)

    @pl.when(j == n_j - 1)
    def _():
        total = jnp.sum(v_buf[0], axis=-1, keepdims=True)
        for jj in range(1, n_j):
            total = total + jnp.sum(v_buf[jj], axis=-1, keepdims=True)
        mu = total * (1.0 / D_MODEL)
        sq = jnp.zeros_like(mu)
        for jj in range(n_j):
            dv = v_buf[jj] - mu
            sq = sq + jnp.sum(dv * dv, axis=-1, keepdims=True)
        rstd = lax.rsqrt(sq * (1.0 / D_MODEL) + EPS)
        row = lax.broadcasted_iota(jnp.int32, (A_CHUNK, A_CHUNK), 0)
        col = lax.broadcasted_iota(jnp.int32, (A_CHUNK, A_CHUNK), 1)
        tril = row >= col
        bst = bst_ref[...]
        for g in range(A_GROUPS):
            ws = jnp.where(tril, ws_ref[g], 0.0).astype(BF16)
            jj, off = divmod(g * cg, tn)
            cs = slice(g * cg, (g + 1) * cg)
            for c in range(tm // A_CHUNK):
                rs = slice(c * A_CHUNK, (c + 1) * A_CHUNK)
                vn = ((v_buf[jj, rs, off:off + cg] - mu[rs]) * rstd[rs] * lng_ref[:, cs]
                      + lnb_ref[:, cs]).astype(BF16)
                mixed = _dot(ws, vn) + bst[:, g:g + 1]
                o_ref[rs, cs] = (p_buf[jj, rs, off:off + cg] * mixed).astype(o_ref.dtype)


def _proj_sgu(h, w_all, layer, ln_g, ln_b, w_s, b_s):
    m, d = h.shape
    tm, tn = min(1024, m), 512
    per = d // tn
    a0 = W_A // tn
    return pl.pallas_call(
        _proj_sgu_kernel,
        grid=(m // tm, per),
        in_specs=[
            pl.BlockSpec((tm, d), lambda i, j: (i, 0)),
            pl.BlockSpec((None, tn, d), lambda i, j: (layer, a0 + j, 0)),
            pl.BlockSpec((None, tn, d), lambda i, j: (layer, a0 + per + j, 0)),
            pl.BlockSpec((None, tn, d), lambda i, j: (layer, a0 + 2 * per + j, 0)),
            pl.BlockSpec((1, d), lambda i, j: (0, 0)),
            pl.BlockSpec((1, d), lambda i, j: (0, 0)),
            pl.BlockSpec((A_GROUPS, A_CHUNK, A_CHUNK), lambda i, j: (0, 0, 0)),
            pl.BlockSpec((A_CHUNK, A_GROUPS), lambda i, j: (0, 0)),
        ],
        out_specs=pl.BlockSpec((tm, d), lambda i, j: (i, 0)),
        out_shape=jax.ShapeDtypeStruct((m, d), BF16),
        scratch_shapes=[pltpu.VMEM((per, tm, tn), F32), pltpu.VMEM((per, tm, tn), BF16)],
        compiler_params=_params("parallel", "arbitrary", vmem=VMEM_LIMIT_BIG_TILES),
        name="proj_sgu",
    )(h, w_all, w_all, w_all, ln_g.reshape(1, d), ln_b.reshape(1, d), w_s, b_s.T)


def _pool_kernel(x_ref, halo_ref, gate_ref, wg_ref, ls_ref, o_ref, ext_ref, *, tiles_per_seq):
    tm = x_ref.shape[0]
    i = pl.program_id(0)
    pos0 = (i % tiles_per_seq) * tm
    halo = halo_ref[...].astype(F32)
    ext_ref[0:POOL_HALO, :] = jnp.where(pos0 == 0, jnp.zeros_like(halo), halo)
    ext_ref[POOL_HALO:POOL_HALO + tm, :] = x_ref[...].astype(F32)
    tpos = pos0 + lax.broadcasted_iota(jnp.int32, (tm, 1), 0)
    for gi, w in enumerate(POOL_WINDOWS):
        cs = slice(gi * C_GROUP, (gi + 1) * C_GROUP)
        x = ext_ref[POOL_HALO:POOL_HALO + tm, cs]
        s = x
        for back in range(1, w):
            s = s + ext_ref[POOL_HALO - back:POOL_HALO - back + tm, cs]
        cnt = jnp.minimum(tpos + 1, w).astype(F32)
        y = (s / cnt - x).astype(BF16)
        yg = (_dot(y, wg_ref[gi].astype(BF16)) * ls_ref[:, cs]
              * jax.nn.silu(gate_ref[:, cs].astype(F32)))
        o_ref[:, cs] = yg.astype(o_ref.dtype)


def _pool(zf, w_grp, ls, seq):
    m = zf.shape[0]
    tm = 256
    d = D_MODEL
    halo_blocks = tm // POOL_HALO
    return pl.pallas_call(
        functools.partial(_pool_kernel, tiles_per_seq=seq // tm),
        grid=(m // tm,),
        in_specs=[
            pl.BlockSpec((tm, d), lambda i: (i, Z_CX // d)),
            pl.BlockSpec((POOL_HALO, d), lambda i: (jnp.maximum(i * halo_blocks - 1, 0), Z_CX // d)),
            pl.BlockSpec((tm, d), lambda i: (i, Z_CG // d)),
            pl.BlockSpec((len(POOL_WINDOWS), C_GROUP, C_GROUP), lambda i: (0, 0, 0)),
            pl.BlockSpec((1, d), lambda i: (0, 0)),
        ],
        out_specs=pl.BlockSpec((tm, d), lambda i: (i, 0)),
        out_shape=jax.ShapeDtypeStruct((m, d), BF16),
        scratch_shapes=[pltpu.VMEM((POOL_HALO + tm, d), F32)],
        compiler_params=_params("parallel"),
        name="pool",
    )(zf, zf, zf, w_grp, ls.reshape(1, d))


def _compress_kernel(x_ref, w1_ref, w2_ref, pos_ref, o_ref, xf_ref):
    n_chunks = x_ref.shape[0] // CMP_STRIDE
    xf_ref[...] = x_ref[...].astype(F32)
    pos = pos_ref[...]
    first = jnp.zeros((n_chunks, HEAD_DIM), F32)
    second = jnp.zeros((n_chunks, HEAD_DIM), F32)
    for l in range(CMP_STRIDE):
        xl = xf_ref[pl.ds(l, n_chunks, stride=CMP_STRIDE), :]
        first += _dot((xl + pos[l:l + 1]).astype(BF16), w1_ref[l].astype(BF16))
        second += _dot((xl + pos[CMP_STRIDE + l:CMP_STRIDE + l + 1]).astype(BF16),
                       w1_ref[CMP_STRIDE + l].astype(BF16))
    hdn = jax.nn.gelu(first + pltpu.roll(second, n_chunks - 1, axis=0))
    o_ref[...] = _dot(hdn.astype(BF16), w2_ref[...].astype(BF16)).astype(o_ref.dtype)


def _compress(z, w1, w2, pos, bsz, seq):
    n_chunks = seq // CMP_STRIDE
    kv0 = Z_KV // HEAD_DIM
    return pl.pallas_call(
        _compress_kernel,
        grid=(bsz, 2, KV_GROUPS),
        in_specs=[
            pl.BlockSpec((seq, HEAD_DIM), lambda b, t, g: (b, kv0 + t * KV_GROUPS + g)),
            pl.BlockSpec((None, CMP_BLOCK, HEAD_DIM, HEAD_DIM), lambda b, t, g: (t, 0, 0, 0)),
            pl.BlockSpec((None, HEAD_DIM, HEAD_DIM), lambda b, t, g: (t, 0, 0)),
            pl.BlockSpec((None, CMP_BLOCK, HEAD_DIM), lambda b, t, g: (t, 0, 0)),
        ],
        out_specs=pl.BlockSpec((None, None, None, n_chunks, HEAD_DIM), lambda b, t, g: (b, t, g, 0, 0)),
        out_shape=jax.ShapeDtypeStruct((bsz, 2, KV_GROUPS, n_chunks, HEAD_DIM), BF16),
        scratch_shapes=[pltpu.VMEM((seq, HEAD_DIM), F32)],
        compiler_params=_params("parallel", "parallel", "parallel"),
        name="compress",
    )(z, w1, w2, pos)


def _bucket_of_distance(n):
    n = np.asarray(n, np.int64)
    max_exact = REL_BUCKETS // 2
    nf = np.maximum(n, 1).astype(np.float64)
    large = max_exact + (np.log(nf / max_exact) / math.log(REL_MAX_DIST / max_exact)
                         * (REL_BUCKETS - max_exact)).astype(np.int64)
    large = np.minimum(large, REL_BUCKETS - 1)
    return np.where(n < max_exact, n, large).astype(np.int32)


def _bias_kernel(tab_ref, idx_tile_ref, idx_cmp_ref, tile_ref, cmp_ref):
    h = pl.program_id(0) * HEADS_PER_GROUP + pl.program_id(1)
    far = tab_ref[REL_BUCKETS - 1, h]
    for idx_ref, out_ref, shift in ((idx_tile_ref, tile_ref, far), (idx_cmp_ref, cmp_ref, 0.0)):
        idx = idx_ref[...]
        acc = jnp.full(idx.shape, NEG_INF, F32)
        for b in range(REL_BUCKETS):
            acc = jnp.where(idx == b, (tab_ref[b, h] - shift) * LOG2_E, acc)
        out_ref[...] = acc


N_BIAS_TILES = 7
MASKED_TILE = N_BIAS_TILES - 1


def _bias_tables(rel_bias, seq):
    t = ATT_TILE
    kt = KEY_TILE
    n_q = seq // t
    r = np.arange(t)[None, :]
    masked = REL_BUCKETS

    def tile(offset, windowed):
        d = offset + r - np.arange(kt)[:, None]
        ok = (d >= 0) & ((d < WINDOW) | (not windowed))
        return np.where(ok, _bucket_of_distance(np.maximum(d, 0)), masked)

    idx_tile = np.stack([tile(0, False), tile(t, False), tile(2 * t, False), tile(3 * t, False),
                         tile(4 * t, True), tile(5 * t, True),
                         np.full((kt, t), masked)]).astype(np.int32)
    assert idx_tile.shape[0] == N_BIAS_TILES and (idx_tile[3] == REL_BUCKETS - 1).all()
    assert (tile(3 * t, True) == idx_tile[3]).all()
    cmp_end = np.arange(t)[:, None] * CMP_STRIDE + CMP_BLOCK - 1
    idx_cmp = np.stack([_bucket_of_distance(np.maximum(i * t + r - cmp_end, 0))
                        for i in range(n_q)]).astype(np.int32)
    return pl.pallas_call(
        _bias_kernel,
        grid=(KV_GROUPS, HEADS_PER_GROUP),
        in_specs=[
            pl.BlockSpec(memory_space=pltpu.SMEM),
            pl.BlockSpec((N_BIAS_TILES, kt, t), lambda g, j: (0, 0, 0)),
            pl.BlockSpec((n_q, t, t), lambda g, j: (0, 0, 0)),
        ],
        out_specs=[
            pl.BlockSpec((None, N_BIAS_TILES, kt, t), lambda g, j: (g, 0, 0, j)),
            pl.BlockSpec((None, n_q, t, t), lambda g, j: (g, 0, 0, j)),
        ],
        out_shape=[
            jax.ShapeDtypeStruct((KV_GROUPS, N_BIAS_TILES, kt, HEADS_PER_GROUP * t), F32),
            jax.ShapeDtypeStruct((KV_GROUPS, n_q, t, HEADS_PER_GROUP * t), F32),
        ],
        compiler_params=_params("parallel", "parallel"),
        name="bias_tables",
    )(rel_bias, jnp.asarray(idx_tile), jnp.asarray(idx_cmp))


def _attn_kernel(q_ref, ks_ref, vs_ref, kw_ref, vw_ref, kc_ref, vc_ref, gsel_ref, bgate_ref,
                 btile_ref, bcmp_ref, ov_ref, expand_ref, eye_ref, o_ref,
                 vst_ref, vwt_ref, kaug_ref, s_ref, w_ref):
    t = ATT_TILE
    kt = KEY_TILE
    hg = HEADS_PER_GROUP
    cols = hg * t
    n_kt = s_ref.shape[0]
    n_q = n_kt * kt // t
    i = pl.program_id(1)
    last_tile = lax.shift_right_logical(i, 1)
    eye = eye_ref[...]

    @pl.when(i == 0)
    def _():
        kaug_ref[:, 0:HEAD_DIM] = ks_ref[...]
        kaug_ref[:, HEAD_DIM:] = expand_ref[...]
        for jt in range(n_kt):
            ks = slice(jt * kt, (jt + 1) * kt)
            vst_ref[jt] = _dot_nt(eye, vs_ref[ks, :]).astype(BF16)
            vwt_ref[jt] = _dot_nt(eye, vw_ref[ks, :]).astype(BF16)

    qb = q_ref[...]
    qs = jnp.concatenate([qb[:, j * HEAD_DIM:(j + 1) * HEAD_DIM] for j in range(hg)], axis=0)

    def bias_index(delta):
        return jnp.where(delta < 0, MASKED_TILE, delta)

    win_tiles = WINDOW // kt + 1
    win_first = jnp.maximum(last_tile - (win_tiles - 1), 0)

    def win_score_tile(js):
        jt = win_first + js
        k = kw_ref[pl.ds(pl.multiple_of(jt * kt, kt), kt), :]
        st = _dot_nt(k, qs) + btile_ref[bias_index(i - 2 * jt)]
        w_ref[js] = st
        return st

    n_idx = lax.broadcasted_iota(jnp.int32, (t, cols), 0)
    tq = i * t + (lax.broadcasted_iota(jnp.int32, (t, cols), 1) & (t - 1))
    valid = tq >= n_idx * CMP_STRIDE + (CMP_BLOCK - 1)
    s = jnp.where(valid, _dot_nt(kc_ref[...], qs) + bcmp_ref[...], NEG_INF)
    win_mx = win_score_tile(0)
    e = jnp.where(valid, jnp.exp2(s - jnp.max(s, axis=0, keepdims=True)), 0.0)
    l = jnp.sum(e, axis=0, keepdims=True)
    p_c = e * (1.0 / jnp.where(l > 0.0, l, 1.0))
    vct = _dot_nt(eye, vc_ref[...]).astype(BF16)
    ocmp_t = _dot(vct, p_c.astype(BF16))

    psum = p_c[:, 0:t]
    for j in range(1, hg):
        psum = psum + p_c[:, j * t:(j + 1) * t]
    p_hi = psum.astype(BF16)
    p_lo = (psum - p_hi.astype(F32)).astype(BF16)
    ov = ov_ref[...]
    n_sel = n_kt * kt // SEL_BLOCK
    imp = (_dot(ov, p_hi) + _dot(ov, p_lo))[0:n_sel]
    win_mx = jnp.maximum(win_mx, win_score_tile(1))
    blk = lax.broadcasted_iota(jnp.int32, (n_sel, t), 0)
    tqb = i * t + lax.broadcasted_iota(jnp.int32, (n_sel, t), 1)
    cur = lax.shift_right_logical(tqb, int(math.log2(SEL_BLOCK)))
    forced = (blk == 0) | (blk == cur) | (blk == cur - 1)
    future = blk * SEL_BLOCK > tqb
    imp = jnp.where(future, -1.0, imp + jnp.where(forced, FORCE_BONUS, 0.0))
    rank = jnp.zeros((n_sel, t), F32)
    for sp in range(n_sel):
        other = imp[sp:sp + 1, :]
        beats = (other > imp) | ((other == imp) & (blk > sp))
        rank = rank + jnp.where(beats, 1.0, 0.0)
    unsel = jnp.where(rank < float(min(N_SELECT, n_sel)), 0.0, 1.0)
    unsel_pad = jnp.concatenate([unsel, jnp.zeros((t - n_sel, t), F32)], axis=0).astype(BF16)
    unsel_qb = _dot_nt(eye, unsel_pad).astype(BF16)
    for js in range(2, win_tiles):
        win_mx = jnp.maximum(win_mx, win_score_tile(js))
    m_win = jnp.max(win_mx, axis=0, keepdims=True)
    q_aug = jnp.concatenate([qs, jnp.concatenate([unsel_qb] * hg, axis=0)], axis=1)

    def value_tile(carry, m, vt, sc_ref, js):
        lsum, acc = carry
        p = jnp.exp2(sc_ref[js] - m)
        return lsum + p, acc + _dot(vt, p.astype(BF16))

    def normalised(carry):
        lsum, acc = carry
        return acc * (1.0 / jnp.sum(lsum, axis=0, keepdims=True))

    zero_carry = (jnp.zeros((kt, cols), F32), jnp.zeros((HEAD_DIM, cols), F32))
    gt = jnp.transpose(jax.nn.sigmoid(gsel_ref[...].astype(F32)))
    out_gate = jax.nn.silu(bgate_ref[...].astype(F32))

    q_per_variant = kt // t
    far_offset = 3
    for variant in range(n_q // q_per_variant):
        n_sub = variant + 1
        n_far = max((q_per_variant * variant - far_offset) // 2 + 1, 0)
        first = max(variant - (win_tiles - 1), 0)

        def sel_bias(js, n_far=n_far):
            if js < n_far:
                return None
            return btile_ref[bias_index(jnp.minimum(i - 2 * js, far_offset))]

        @pl.when(last_tile == variant)
        def _(n_sub=n_sub, sel_bias=sel_bias, first=first):
            mx = jnp.full((kt, cols), NEG_INF, F32)
            for js in range(n_sub):
                st = _dot_nt(kaug_ref[js * kt:(js + 1) * kt, :], q_aug)
                bias = sel_bias(js)
                if bias is not None:
                    st = st + bias
                s_ref[js] = st
                mx = jnp.maximum(mx, st)
            win = zero_carry
            for js in range(win_tiles):
                win = value_tile(win, m_win, vwt_ref[first + js], w_ref, js)
            owin_t = normalised(win)
            m_sel = jnp.max(mx, axis=0, keepdims=True)
            sel = zero_carry
            part = []
            for j in range(hg):
                cs = slice(j * t, (j + 1) * t)
                part.append(jnp.transpose(gt[j:j + 1] * ocmp_t[:, cs]
                                          + gt[2 * hg + j:2 * hg + j + 1] * owin_t[:, cs]))
            for js in range(n_sub):
                sel = value_tile(sel, m_sel, vst_ref[js], s_ref, js)
            osel_t = normalised(sel)
            for j in range(hg):
                cs = slice(j * t, (j + 1) * t)
                hs = slice(j * HEAD_DIM, (j + 1) * HEAD_DIM)
                o = part[j] + jnp.transpose(gt[hg + j:hg + j + 1] * osel_t[:, cs])
                o_ref[:, hs] = (o * out_gate[:, hs]).astype(o_ref.dtype)


def _attention_constants(seq):
    t = ATT_TILE
    n_cmp_pad = seq // CMP_STRIDE
    assert n_cmp_pad == t, "one tile of compressed keys per sequence"
    n_cmp = n_cmp_pad - CMP_BLOCK // CMP_STRIDE + 1
    n_sel = seq // SEL_BLOCK
    cst = np.arange(n_cmp_pad) * CMP_STRIDE
    sst = np.arange(n_sel) * SEL_BLOCK
    overlap = ((cst[:, None] < sst[None] + SEL_BLOCK) & (cst[:, None] + CMP_BLOCK > sst[None]))
    overlap[n_cmp:] = False
    ov = np.zeros((t, n_cmp_pad), np.float32)
    ov[:n_sel] = overlap.T
    keys = np.arange(seq)
    expand = np.zeros((seq, t), np.float32)
    expand[keys, keys // SEL_BLOCK] = NEG_INF
    eye = np.eye(t, dtype=np.float32)
    return (jnp.asarray(ov, BF16), jnp.asarray(expand, BF16), jnp.asarray(eye, BF16))


def _attention(z, kvc, bias_tile, bias_cmp, consts, bsz, seq):
    t = ATT_TILE
    kt = KEY_TILE
    n_q = seq // t
    n_kt = seq // kt
    gw = HEADS_PER_GROUP * HEAD_DIM
    kv0 = Z_KV // HEAD_DIM
    ov, expand, eye = consts

    def kv_spec(kind):
        return pl.BlockSpec((seq, HEAD_DIM),
                            lambda bg, i: (bg % bsz, kv0 + kind * KV_GROUPS + bg // bsz))

    def cmp_spec(kind):
        return pl.BlockSpec((None, None, None, t, HEAD_DIM),
                            lambda bg, i: (bg % bsz, kind, bg // bsz, 0, 0))

    def tok(bg, i):
        return (bg % bsz) * n_q + i

    return pl.pallas_call(
        _attn_kernel,
        grid=(bsz * KV_GROUPS, n_q),
        in_specs=[
            pl.BlockSpec((t, gw), lambda bg, i: (tok(bg, i), Z_Q // gw + bg // bsz)),
            kv_spec(KV_KINDS.index("k_s")), kv_spec(KV_KINDS.index("v_s")),
            kv_spec(KV_KINDS.index("k_w")), kv_spec(KV_KINDS.index("v_w")),
            cmp_spec(0), cmp_spec(1),
            pl.BlockSpec((t, V7X_LANES), lambda bg, i: (tok(bg, i), Z_GSEL // V7X_LANES + bg // bsz)),
            pl.BlockSpec((t, gw), lambda bg, i: (tok(bg, i), Z_BG // gw + bg // bsz)),
            pl.BlockSpec((None, N_BIAS_TILES, kt, gw), lambda bg, i: (bg // bsz, 0, 0, 0)),
            pl.BlockSpec((None, None, t, gw), lambda bg, i: (bg // bsz, i, 0, 0)),
            pl.BlockSpec((t, t), lambda bg, i: (0, 0)),
            pl.BlockSpec((seq, t), lambda bg, i: (0, 0)),
            pl.BlockSpec((t, t), lambda bg, i: (0, 0)),
        ],
        out_specs=pl.BlockSpec((t, gw), lambda bg, i: (tok(bg, i), bg // bsz)),
        out_shape=jax.ShapeDtypeStruct((bsz * seq, D_MODEL), BF16),
        scratch_shapes=[pltpu.VMEM((n_kt, HEAD_DIM, kt), BF16),
                        pltpu.VMEM((n_kt, HEAD_DIM, kt), BF16),
                        pltpu.VMEM((seq, HEAD_DIM + t), BF16),
                        pltpu.VMEM((n_kt, kt, gw), F32),
                        pltpu.VMEM((WINDOW // kt + 1, kt, gw), F32)],
        compiler_params=_params("parallel", "arbitrary"),
        name="nsa_attention",
    )(z, z, z, z, z, kvc, kvc, z, z, bias_tile, bias_cmp, ov, expand, eye)


def _merge_kernel(ya_ref, yb_ref, yc_ref, ma_ref, mb_ref, mc_ref, w_ref, o_ref):
    y = jax.nn.sigmoid(ma_ref[...].astype(F32)) * _dot(ya_ref[...], w_ref[0])
    y += jax.nn.sigmoid(mb_ref[...].astype(F32)) * _dot(yb_ref[...], w_ref[1])
    y += jax.nn.sigmoid(mc_ref[...].astype(F32)) * _dot(yc_ref[...], w_ref[2])
    o_ref[...] = y.astype(o_ref.dtype)


def _merge(ya, yb, yc, zf, w_branch):
    m, d = ya.shape
    tm, tn = min(1024, m), 512
    m0 = Z_MERGE // tn
    per = d // tn
    y_spec = pl.BlockSpec((tm, d), lambda i, j: (i, 0))
    return pl.pallas_call(
        _merge_kernel,
        grid=(m // tm, d // tn),
        in_specs=[
            y_spec, y_spec, y_spec,
            pl.BlockSpec((tm, tn), lambda i, j: (i, m0 + j)),
            pl.BlockSpec((tm, tn), lambda i, j: (i, m0 + per + j)),
            pl.BlockSpec((tm, tn), lambda i, j: (i, m0 + 2 * per + j)),
            pl.BlockSpec((3, d, tn), lambda i, j: (0, 0, j)),
        ],
        out_specs=pl.BlockSpec((tm, tn), lambda i, j: (i, j)),
        out_shape=jax.ShapeDtypeStruct((m, d), BF16),
        compiler_params=_params("parallel", "arbitrary", vmem=VMEM_LIMIT_BIG_TILES),
        name="merge",
    )(ya, yb, yc, zf, zf, zf, w_branch)


def _out_kernel(y_ref, w_ref, x_ref, mod_ref, g_ref, *rest, last):
    x = x_ref[...] + mod_ref[2:3, :] * _dot(y_ref[...], w_ref[...])
    normed = x * lax.rsqrt(jnp.mean(x * x, axis=-1, keepdims=True) + EPS) * g_ref[...]
    if last:
        (o_ref,) = rest
        o_ref[...] = normed
    else:
        next_mod_ref, x_out_ref, h_ref = rest
        nm = next_mod_ref[...]
        x_out_ref[...] = x
        h_ref[...] = (normed * (1.0 + nm[1:2]) + nm[0:1]).astype(h_ref.dtype)


def _out_proj(y, w_out, x2, mod3, g, next_mod3, seq):
    m, d = y.shape
    tm = 512
    per_seq = seq // tm
    last = next_mod3 is None
    row = pl.BlockSpec((tm, d), lambda i: (i, 0))
    mod_spec = pl.BlockSpec((None, 3, d), lambda i: (i // per_seq, 0, 0))
    in_specs = [row, pl.BlockSpec((d, d), lambda i: (0, 0)), row, mod_spec,
                pl.BlockSpec((1, d), lambda i: (0, 0))]
    args = [y, w_out, x2, mod3, g.reshape(1, d)]
    if last:
        out_specs = row
        out_shape = jax.ShapeDtypeStruct((m, d), F32)
    else:
        in_specs.append(mod_spec)
        args.append(next_mod3)
        out_specs = [row, row]
        out_shape = [jax.ShapeDtypeStruct((m, d), F32), jax.ShapeDtypeStruct((m, d), BF16)]
    return pl.pallas_call(
        functools.partial(_out_kernel, last=last),
        grid=(m // tm,),
        in_specs=in_specs,
        out_specs=out_specs,
        out_shape=out_shape,
        compiler_params=_params("parallel"),
        name="out_proj",
    )(*args)


def _prep_weights(w_in):
    depth, d, n_in = w_in.shape
    kvw = KV_GROUPS * HEAD_DIM
    wt = jnp.swapaxes(w_in, 1, 2)
    sizes = (d, d, d, d, 6 * kvw, 3 * N_HEADS, d, d, d, 3 * d)
    offs = np.concatenate([[0], np.cumsum(sizes)])
    a_end, q0, kv0, gsel0, tail0 = offs[3], offs[3], offs[4], offs[5], offs[6]
    gs = wt[:, gsel0:tail0].reshape(depth, 3, KV_GROUPS, HEADS_PER_GROUP, d).transpose(0, 2, 1, 3, 4)
    gs = gs.reshape(depth, KV_GROUPS, 3 * HEADS_PER_GROUP, d)
    gs = jnp.pad(gs, ((0, 0), (0, 0), (0, V7X_LANES - 3 * HEADS_PER_GROUP), (0, 0)))
    gs = gs.reshape(depth, KV_GROUPS * V7X_LANES, d)
    q_scale = HEAD_DIM ** -0.5 * LOG2_E
    w_all = jnp.concatenate([wt[:, q0:kv0] * q_scale, wt[:, tail0:], wt[:, kv0:gsel0], gs,
                             wt[:, :a_end]], axis=1)
    assert w_all.shape[1] == W_COLS
    return w_all.astype(BF16)


def kernel(x, c, rel_bias, norm_g, w_ada, b_ada, w_in, a_ln_g, a_ln_b, a_w_s, a_b_s, b_w_cmp1,
           b_w_cmp2, b_pos_cmp, c_w_grp, c_scale, w_branch, w_out, final_g):
    bsz, seq, d = x.shape
    depth = w_in.shape[0]
    assert d == D_MODEL and seq % 512 == 0
    x2 = x.reshape(bsz * seq, d)
    mod = _ada(c, w_ada, b_ada)
    bias_tile, bias_cmp = _bias_tables(rel_bias, seq)
    consts = _attention_constants(seq)
    mod3 = [mod[l].reshape(bsz, 3, d) for l in range(depth)]
    h = _prenorm(x2, norm_g[0], mod3[0], seq)
    w_all = _prep_weights(w_in)
    for l in range(depth):
        z = _proj(h, w_all, l, Z_COLS, BF16)
        ya = _proj_sgu(h, w_all, l, a_ln_g[l], a_ln_b[l], a_w_s[l], a_b_s[l])
        yc = _pool(z, c_w_grp[l], c_scale[l], seq)
        kvc = _compress(z, b_w_cmp1[l], b_w_cmp2[l], b_pos_cmp[l], bsz, seq)
        yb = _attention(z, kvc, bias_tile, bias_cmp, consts, bsz, seq)
        y = _merge(ya, yb, yc, z, w_branch[l].astype(BF16))
        if l + 1 < depth:
            x2, h = _out_proj(y, w_out[l].astype(BF16), x2, mod3[l], norm_g[l + 1], mod3[l + 1], seq)
        else:
            out = _out_proj(y, w_out[l].astype(BF16), x2, mod3[l], final_g, None, seq)
    return out.reshape(bsz, seq, d)
```

```python
import functools
import math

import numpy as np
import jax
import jax.numpy as jnp
from jax import lax
from jax.experimental import pallas as pl
from jax.experimental.pallas import tpu as pltpu

D_MODEL = 2048
A_GROUPS = 8
A_CHUNK = 128
HEAD_DIM = 128
N_HEADS = 16
KV_GROUPS = 4
HEADS_PER_GROUP = N_HEADS // KV_GROUPS
CMP_BLOCK = 32
CMP_STRIDE = 16
SEL_BLOCK = 64
N_SELECT = 16
WINDOW = 512
FORCE_BONUS = 1e4
POOL_WINDOWS = (2, 4, 8, 16)
C_GROUP = D_MODEL // len(POOL_WINDOWS)
REL_BUCKETS = 32
REL_MAX_DIST = 128
EPS = 1e-6
NEG_INF = -1e30

V7X_LANES = 128
V7X_VMEM_BYTES = 64 * 1024 * 1024
VMEM_LIMIT = 48 * 1024 * 1024
VMEM_LIMIT_BIG_TILES = 56 * 1024 * 1024

ATT_TILE = 128
KEY_TILE = 256
LOG2_E = math.log2(math.e)
POOL_HALO = 16
POOL_PAD = 32
BF16 = jnp.bfloat16
F32 = jnp.float32

Z_Q = 0
Z_BG = 2048
Z_CX, Z_CG = 4096, 6144
Z_MERGE = 8192
Z_KV = 14336
Z_GSEL = 17408
Z_COLS = 17920
W_A = Z_COLS
W_COLS = W_A + 3 * D_MODEL
KV_KINDS = ("k_c", "v_c", "k_s", "v_s", "k_w", "v_w")


def _dot(a, b):
    return jnp.dot(a, b, preferred_element_type=F32)


def _dot_nt(a, b):
    return lax.dot_general(a, b, (((1,), (1,)), ((), ())), preferred_element_type=F32)


def _params(*sem, vmem=VMEM_LIMIT):
    return pltpu.CompilerParams(dimension_semantics=sem, vmem_limit_bytes=vmem)


def _ada_kernel(c_ref, w_ref, b_ref, o_ref):
    a = jax.nn.silu(c_ref[...]).astype(BF16)
    o_ref[...] = _dot(a, w_ref[...].astype(BF16)) + b_ref[...]


def _ada(c, w_ada, b_ada):
    depth, d, n = w_ada.shape
    bsz = c.shape[0]
    tn = 1024
    return pl.pallas_call(
        _ada_kernel,
        grid=(depth, n // tn),
        in_specs=[
            pl.BlockSpec((bsz, d), lambda l, j: (0, 0)),
            pl.BlockSpec((None, d, tn), lambda l, j: (l, 0, j)),
            pl.BlockSpec((None, 1, tn), lambda l, j: (l, 0, j)),
        ],
        out_specs=pl.BlockSpec((None, bsz, tn), lambda l, j: (l, 0, j)),
        out_shape=jax.ShapeDtypeStruct((depth, bsz, n), F32),
        compiler_params=_params("parallel", "parallel"),
        name="ada_mod",
    )(c, w_ada, b_ada.reshape(depth, 1, n))


def _prenorm_kernel(x_ref, g_ref, mod_ref, o_ref):
    x = x_ref[...]
    y = x * lax.rsqrt(jnp.mean(x * x, axis=-1, keepdims=True) + EPS) * g_ref[...]
    m = mod_ref[...]
    o_ref[...] = (y * (1.0 + m[1:2]) + m[0:1]).astype(o_ref.dtype)


def _prenorm(x2, g, mod3, seq):
    m, d = x2.shape
    tm = 512
    per_seq = seq // tm
    return pl.pallas_call(
        _prenorm_kernel,
        grid=(m // tm,),
        in_specs=[
            pl.BlockSpec((tm, d), lambda i: (i, 0)),
            pl.BlockSpec((1, d), lambda i: (0, 0)),
            pl.BlockSpec((None, 3, d), lambda i: (i // per_seq, 0, 0)),
        ],
        out_specs=pl.BlockSpec((tm, d), lambda i: (i, 0)),
        out_shape=jax.ShapeDtypeStruct((m, d), BF16),
        compiler_params=_params("parallel"),
        name="prenorm",
    )(x2, g.reshape(1, d), mod3)


def _proj_kernel(a_ref, w_ref, o_ref):
    o_ref[...] = _dot_nt(a_ref[...], w_ref[...]).astype(o_ref.dtype)


def _proj(a, w_all, layer, n, out_dtype, tm=1024, tn=2560):
    m, k = a.shape
    tm = min(tm, m)
    return pl.pallas_call(
        _proj_kernel,
        grid=(m // tm, n // tn),
        in_specs=[
            pl.BlockSpec((tm, k), lambda i, j: (i, 0)),
            pl.BlockSpec((None, tn, k), lambda i, j: (layer, j, 0)),
        ],
        out_specs=pl.BlockSpec((tm, tn), lambda i, j: (i, j)),
        out_shape=jax.ShapeDtypeStruct((m, n), out_dtype),
        compiler_params=_params("parallel", "arbitrary"),
        name="in_proj",
    )(a, w_all)


def _proj_sgu_kernel(h_ref, wu_ref, wv_ref, wg_ref, lng_ref, lnb_ref, ws_ref, bst_ref, o_ref,
                     v_buf, p_buf):
    j = pl.program_id(1)
    n_j, tm, tn = v_buf.shape
    cg = D_MODEL // A_GROUPS
    h = h_ref[...]
    v_buf[j] = jax.nn.gelu(_dot_nt(h, wv_ref[...]))
    p_buf[j] = (jax.nn.gelu(_dot_nt(h, wu_ref[...]))
                * jax.nn.silu(_dot_nt(h, wg_ref[...]))).astype(p_buf.dtype)

    @pl.when(j == n_j - 1)
    def _():
        total = jnp.sum(v_buf[0], axis=-1, keepdims=True)
        for jj in range(1, n_j):
            total = total + jnp.sum(v_buf[jj], axis=-1, keepdims=True)
        mu = total * (1.0 / D_MODEL)
        sq = jnp.zeros_like(mu)
        for jj in range(n_j):
            dv = v_buf[jj] - mu
            sq = sq + jnp.sum(dv * dv, axis=-1, keepdims=True)
        rstd = lax.rsqrt(sq * (1.0 / D_MODEL) + EPS)
        row = lax.broadcasted_iota(jnp.int32, (A_CHUNK, A_CHUNK), 0)
        col = lax.broadcasted_iota(jnp.int32, (A_CHUNK, A_CHUNK), 1)
        tril = row >= col
        bst = bst_ref[...]
        for g in range(A_GROUPS):
            ws = jnp.where(tril, ws_ref[g], 0.0).astype(BF16)
            jj, off = divmod(g * cg, tn)
            cs = slice(g * cg, (g + 1) * cg)
            for c in range(tm // A_CHUNK):
                rs = slice(c * A_CHUNK, (c + 1) * A_CHUNK)
                vn = ((v_buf[jj, rs, off:off + cg] - mu[rs]) * rstd[rs] * lng_ref[:, cs]
                      + lnb_ref[:, cs]).astype(BF16)
                mixed = _dot(ws, vn) + bst[:, g:g + 1]
                o_ref[rs, cs] = (p_buf[jj, rs, off:off + cg] * mixed).astype(o_ref.dtype)


def _proj_sgu(h, w_all, layer, ln_g, ln_b, w_s, b_s):
    m, d = h.shape
    tm, tn = min(1024, m), 512
    per = d // tn
    a0 = W_A // tn
    return pl.pallas_call(
        _proj_sgu_kernel,
        grid=(m // tm, per),
        in_specs=[
            pl.BlockSpec((tm, d), lambda i, j: (i, 0)),
            pl.BlockSpec((None, tn, d), lambda i, j: (layer, a0 + j, 0)),
            pl.BlockSpec((None, tn, d), lambda i, j: (layer, a0 + per + j, 0)),
            pl.BlockSpec((None, tn, d), lambda i, j: (layer, a0 + 2 * per + j, 0)),
            pl.BlockSpec((1, d), lambda i, j: (0, 0)),
            pl.BlockSpec((1, d), lambda i, j: (0, 0)),
            pl.BlockSpec((A_GROUPS, A_CHUNK, A_CHUNK), lambda i, j: (0, 0, 0)),
            pl.BlockSpec((A_CHUNK, A_GROUPS), lambda i, j: (0, 0)),
        ],
        out_specs=pl.BlockSpec((tm, d), lambda i, j: (i, 0)),
        out_shape=jax.ShapeDtypeStruct((m, d), BF16),
        scratch_shapes=[pltpu.VMEM((per, tm, tn), F32), pltpu.VMEM((per, tm, tn), BF16)],
        compiler_params=_params("parallel", "arbitrary", vmem=VMEM_LIMIT_BIG_TILES),
        name="proj_sgu",
    )(h, w_all, w_all, w_all, ln_g.reshape(1, d), ln_b.reshape(1, d), w_s, b_s.T)


def _pool_kernel(x_ref, halo_ref, gate_ref, wg_ref, ls_ref, o_ref, ext_ref, lvl_ref, *,
                 tiles_per_seq):
    tm = x_ref.shape[0]
    i = pl.program_id(0)
    pos0 = (i % tiles_per_seq) * tm
    halo = halo_ref[...].astype(F32)
    ext_ref[0:POOL_PAD - POOL_HALO, :] = jnp.zeros((POOL_PAD - POOL_HALO, ext_ref.shape[1]), F32)
    ext_ref[POOL_PAD - POOL_HALO:POOL_PAD, :] = jnp.where(pos0 == 0, jnp.zeros_like(halo), halo)
    ext_ref[POOL_PAD:POOL_PAD + tm, :] = x_ref[...].astype(F32)
    tpos = pos0 + lax.broadcasted_iota(jnp.int32, (tm, 1), 0)
    end = POOL_PAD + tm
    for gi, w in enumerate(POOL_WINDOWS):
        cs = slice(gi * C_GROUP, (gi + 1) * C_GROUP)
        x = ext_ref[POOL_PAD:end, cs]
        levels = w.bit_length() - 1
        src = ext_ref
        for k in range(1, levels + 1):
            shift = 1 << (k - 1)
            lo = POOL_PAD - 8 * (levels - k)
            s = src[lo:end, cs] + src[lo - shift:end - shift, cs]
            if k < levels:
                lvl_ref[k - 1, lo:end, cs] = s
                src = lvl_ref.at[k - 1]
        cnt = jnp.minimum(tpos + 1, w).astype(F32)
        y = (s / cnt - x).astype(BF16)
        yg = (_dot(y, wg_ref[gi].astype(BF16)) * ls_ref[:, cs]
              * jax.nn.silu(gate_ref[:, cs].astype(F32)))
        o_ref[:, cs] = yg.astype(o_ref.dtype)


def _pool(zf, w_grp, ls, seq):
    m = zf.shape[0]
    tm = 256
    d = D_MODEL
    halo_blocks = tm // POOL_HALO
    return pl.pallas_call(
        functools.partial(_pool_kernel, tiles_per_seq=seq // tm),
        grid=(m // tm,),
        in_specs=[
            pl.BlockSpec((tm, d), lambda i: (i, Z_CX // d)),
            pl.BlockSpec((POOL_HALO, d), lambda i: (jnp.maximum(i * halo_blocks - 1, 0), Z_CX // d)),
            pl.BlockSpec((tm, d), lambda i: (i, Z_CG // d)),
            pl.BlockSpec((len(POOL_WINDOWS), C_GROUP, C_GROUP), lambda i: (0, 0, 0)),
            pl.BlockSpec((1, d), lambda i: (0, 0)),
        ],
        out_specs=pl.BlockSpec((tm, d), lambda i: (i, 0)),
        out_shape=jax.ShapeDtypeStruct((m, d), BF16),
        scratch_shapes=[pltpu.VMEM((POOL_PAD + tm, d), F32),
                        pltpu.VMEM((max(POOL_WINDOWS).bit_length() - 2, POOL_PAD + tm, d), F32)],
        compiler_params=_params("parallel"),
        name="pool",
    )(zf, zf, zf, w_grp, ls.reshape(1, d))


def _compress_kernel(x_ref, w1_ref, w2_ref, pos_ref, o_ref, xf_ref):
    n_chunks = x_ref.shape[0] // CMP_STRIDE
    xf_ref[...] = x_ref[...].astype(F32)
    pos = pos_ref[...]
    first = jnp.zeros((n_chunks, HEAD_DIM), F32)
    second = jnp.zeros((n_chunks, HEAD_DIM), F32)
    for l in range(CMP_STRIDE):
        xl = xf_ref[pl.ds(l, n_chunks, stride=CMP_STRIDE), :]
        first += _dot((xl + pos[l:l + 1]).astype(BF16), w1_ref[l].astype(BF16))
        second += _dot((xl + pos[CMP_STRIDE + l:CMP_STRIDE + l + 1]).astype(BF16),
                       w1_ref[CMP_STRIDE + l].astype(BF16))
    hdn = jax.nn.gelu(first + pltpu.roll(second, n_chunks - 1, axis=0))
    o_ref[...] = _dot(hdn.astype(BF16), w2_ref[...].astype(BF16)).astype(o_ref.dtype)


def _compress(z, w1, w2, pos, bsz, seq):
    n_chunks = seq // CMP_STRIDE
    kv0 = Z_KV // HEAD_DIM
    return pl.pallas_call(
        _compress_kernel,
        grid=(bsz, 2, KV_GROUPS),
        in_specs=[
            pl.BlockSpec((seq, HEAD_DIM), lambda b, t, g: (b, kv0 + t * KV_GROUPS + g)),
            pl.BlockSpec((None, CMP_BLOCK, HEAD_DIM, HEAD_DIM), lambda b, t, g: (t, 0, 0, 0)),
            pl.BlockSpec((None, HEAD_DIM, HEAD_DIM), lambda b, t, g: (t, 0, 0)),
            pl.BlockSpec((None, CMP_BLOCK, HEAD_DIM), lambda b, t, g: (t, 0, 0)),
        ],
        out_specs=pl.BlockSpec((None, None, None, n_chunks, HEAD_DIM), lambda b, t, g: (b, t, g, 0, 0)),
        out_shape=jax.ShapeDtypeStruct((bsz, 2, KV_GROUPS, n_chunks, HEAD_DIM), BF16),
        scratch_shapes=[pltpu.VMEM((seq, HEAD_DIM), F32)],
        compiler_params=_params("parallel", "parallel", "parallel"),
        name="compress",
    )(z, w1, w2, pos)


def _bucket_of_distance(n):
    n = np.asarray(n, np.int64)
    max_exact = REL_BUCKETS // 2
    nf = np.maximum(n, 1).astype(np.float64)
    large = max_exact + (np.log(nf / max_exact) / math.log(REL_MAX_DIST / max_exact)
                         * (REL_BUCKETS - max_exact)).astype(np.int64)
    large = np.minimum(large, REL_BUCKETS - 1)
    return np.where(n < max_exact, n, large).astype(np.int32)


def _bias_kernel(tab_ref, idx_tile_ref, idx_cmp_ref, tile_ref, cmp_ref):
    h = pl.program_id(0) * HEADS_PER_GROUP + pl.program_id(1)
    far = tab_ref[REL_BUCKETS - 1, h]
    for idx_ref, out_ref, shift in ((idx_tile_ref, tile_ref, far), (idx_cmp_ref, cmp_ref, 0.0)):
        idx = idx_ref[...]
        acc = jnp.full(idx.shape, NEG_INF, F32)
        for b in range(REL_BUCKETS):
            acc = jnp.where(idx == b, (tab_ref[b, h] - shift) * LOG2_E, acc)
        out_ref[...] = acc


N_BIAS_TILES = 7
MASKED_TILE = N_BIAS_TILES - 1


def _bias_tables(rel_bias, seq):
    t = ATT_TILE
    kt = KEY_TILE
    n_q = seq // t
    r = np.arange(t)[None, :]
    masked = REL_BUCKETS

    def tile(offset, windowed):
        d = offset + r - np.arange(kt)[:, None]
        ok = (d >= 0) & ((d < WINDOW) | (not windowed))
        return np.where(ok, _bucket_of_distance(np.maximum(d, 0)), masked)

    idx_tile = np.stack([tile(0, False), tile(t, False), tile(2 * t, False), tile(3 * t, False),
                         tile(4 * t, True), tile(5 * t, True),
                         np.full((kt, t), masked)]).astype(np.int32)
    assert idx_tile.shape[0] == N_BIAS_TILES and (idx_tile[3] == REL_BUCKETS - 1).all()
    assert (tile(3 * t, True) == idx_tile[3]).all()
    cmp_end = np.arange(t)[:, None] * CMP_STRIDE + CMP_BLOCK - 1
    idx_cmp = np.stack([_bucket_of_distance(np.maximum(i * t + r - cmp_end, 0))
                        for i in range(n_q)]).astype(np.int32)
    return pl.pallas_call(
        _bias_kernel,
        grid=(KV_GROUPS, HEADS_PER_GROUP),
        in_specs=[
            pl.BlockSpec(memory_space=pltpu.SMEM),
            pl.BlockSpec((N_BIAS_TILES, kt, t), lambda g, j: (0, 0, 0)),
            pl.BlockSpec((n_q, t, t), lambda g, j: (0, 0, 0)),
        ],
        out_specs=[
            pl.BlockSpec((None, N_BIAS_TILES, kt, t), lambda g, j: (g, 0, 0, j)),
            pl.BlockSpec((None, n_q, t, t), lambda g, j: (g, 0, 0, j)),
        ],
        out_shape=[
            jax.ShapeDtypeStruct((KV_GROUPS, N_BIAS_TILES, kt, HEADS_PER_GROUP * t), F32),
            jax.ShapeDtypeStruct((KV_GROUPS, n_q, t, HEADS_PER_GROUP * t), F32),
        ],
        compiler_params=_params("parallel", "parallel"),
        name="bias_tables",
    )(rel_bias, jnp.asarray(idx_tile), jnp.asarray(idx_cmp))


def _attn_kernel(q_ref, ks_ref, vs_ref, kw_ref, vw_ref, kc_ref, vc_ref, gsel_ref, bgate_ref,
                 btile_ref, bcmp_ref, ov_ref, expand_ref, eye_ref, o_ref,
                 vst_ref, vwt_ref, kaug_ref, s_ref, w_ref):
    t = ATT_TILE
    kt = KEY_TILE
    hg = HEADS_PER_GROUP
    cols = hg * t
    n_kt = s_ref.shape[0]
    n_q = n_kt * kt // t
    i = pl.program_id(1)
    last_tile = lax.shift_right_logical(i, 1)
    eye = eye_ref[...]

    @pl.when(i == 0)
    def _():
        kaug_ref[:, 0:HEAD_DIM] = ks_ref[...]
        kaug_ref[:, HEAD_DIM:] = expand_ref[...]
        for jt in range(n_kt):
            ks = slice(jt * kt, (jt + 1) * kt)
            vst_ref[jt] = _dot_nt(eye, vs_ref[ks, :]).astype(BF16)
            vwt_ref[jt] = _dot_nt(eye, vw_ref[ks, :]).astype(BF16)

    qb = q_ref[...]
    qs = jnp.concatenate([qb[:, j * HEAD_DIM:(j + 1) * HEAD_DIM] for j in range(hg)], axis=0)

    def bias_index(delta):
        return jnp.where(delta < 0, MASKED_TILE, delta)

    win_tiles = WINDOW // kt + 1
    win_first = jnp.maximum(last_tile - (win_tiles - 1), 0)

    def win_score_tile(js):
        jt = win_first + js
        k = kw_ref[pl.ds(pl.multiple_of(jt * kt, kt), kt), :]
        st = _dot_nt(k, qs) + btile_ref[bias_index(i - 2 * jt)]
        w_ref[js] = st
        return st

    n_idx = lax.broadcasted_iota(jnp.int32, (t, cols), 0)
    tq = i * t + (lax.broadcasted_iota(jnp.int32, (t, cols), 1) & (t - 1))
    valid = tq >= n_idx * CMP_STRIDE + (CMP_BLOCK - 1)
    s = jnp.where(valid, _dot_nt(kc_ref[...], qs) + bcmp_ref[...], NEG_INF)
    win_mx = win_score_tile(0)
    e = jnp.where(valid, jnp.exp2(s - jnp.max(s, axis=0, keepdims=True)), 0.0)
    l = jnp.sum(e, axis=0, keepdims=True)
    p_c = e * (1.0 / jnp.where(l > 0.0, l, 1.0))
    vct = _dot_nt(eye, vc_ref[...]).astype(BF16)
    ocmp_t = _dot(vct, p_c.astype(BF16))

    psum = p_c[:, 0:t]
    for j in range(1, hg):
        psum = psum + p_c[:, j * t:(j + 1) * t]
    p_hi = psum.astype(BF16)
    p_lo = (psum - p_hi.astype(F32)).astype(BF16)
    ov = ov_ref[...]
    n_sel = n_kt * kt // SEL_BLOCK
    imp = (_dot(ov, p_hi) + _dot(ov, p_lo))[0:n_sel]
    win_mx = jnp.maximum(win_mx, win_score_tile(1))
    blk = lax.broadcasted_iota(jnp.int32, (n_sel, t), 0)
    tqb = i * t + lax.broadcasted_iota(jnp.int32, (n_sel, t), 1)
    cur = lax.shift_right_logical(tqb, int(math.log2(SEL_BLOCK)))
    forced = (blk == 0) | (blk == cur) | (blk == cur - 1)
    future = blk * SEL_BLOCK > tqb
    imp = jnp.where(future, -1.0, imp + jnp.where(forced, FORCE_BONUS, 0.0))
    rank = jnp.zeros((n_sel, t), F32)
    for sp in range(n_sel):
        other = imp[sp:sp + 1, :]
        beats = (other > imp) | ((other == imp) & (blk > sp))
        rank = rank + jnp.where(beats, 1.0, 0.0)
    unsel = jnp.where(rank < float(min(N_SELECT, n_sel)), 0.0, 1.0)
    unsel_pad = jnp.concatenate([unsel, jnp.zeros((t - n_sel, t), F32)], axis=0).astype(BF16)
    unsel_qb = _dot_nt(eye, unsel_pad).astype(BF16)
    for js in range(2, win_tiles):
        win_mx = jnp.maximum(win_mx, win_score_tile(js))
    m_win = jnp.max(win_mx, axis=0, keepdims=True)
    q_aug = jnp.concatenate([qs, jnp.concatenate([unsel_qb] * hg, axis=0)], axis=1)

    def value_tile(carry, m, vt, sc_ref, js):
        lsum, acc = carry
        p = jnp.exp2(sc_ref[js] - m)
        return lsum + p, acc + _dot(vt, p.astype(BF16))

    def normalised(carry):
        lsum, acc = carry
        return acc * (1.0 / jnp.sum(lsum, axis=0, keepdims=True))

    zero_carry = (jnp.zeros((kt, cols), F32), jnp.zeros((HEAD_DIM, cols), F32))
    gt = jnp.transpose(jax.nn.sigmoid(gsel_ref[...].astype(F32)))
    out_gate = jax.nn.silu(bgate_ref[...].astype(F32))

    q_per_variant = kt // t
    far_offset = 3
    for variant in range(n_q // q_per_variant):
        n_sub = variant + 1
        n_far = max((q_per_variant * variant - far_offset) // 2 + 1, 0)
        first = max(variant - (win_tiles - 1), 0)

        def sel_bias(js, n_far=n_far):
            if js < n_far:
                return None
            return btile_ref[bias_index(jnp.minimum(i - 2 * js, far_offset))]

        @pl.when(last_tile == variant)
        def _(n_sub=n_sub, sel_bias=sel_bias, first=first):
            mx = jnp.full((kt, cols), NEG_INF, F32)
            for js in range(n_sub):
                st = _dot_nt(kaug_ref[js * kt:(js + 1) * kt, :], q_aug)
                bias = sel_bias(js)
                if bias is not None:
                    st = st + bias
                s_ref[js] = st
                mx = jnp.maximum(mx, st)
            win = zero_carry
            for js in range(win_tiles):
                win = value_tile(win, m_win, vwt_ref[first + js], w_ref, js)
            owin_t = normalised(win)
            m_sel = jnp.max(mx, axis=0, keepdims=True)
            sel = zero_carry
            part = []
            for j in range(hg):
                cs = slice(j * t, (j + 1) * t)
                part.append(jnp.transpose(gt[j:j + 1] * ocmp_t[:, cs]
                                          + gt[2 * hg + j:2 * hg + j + 1] * owin_t[:, cs]))
            for js in range(n_sub):
                sel = value_tile(sel, m_sel, vst_ref[js], s_ref, js)
            osel_t = normalised(sel)
            for j in range(hg):
                cs = slice(j * t, (j + 1) * t)
                hs = slice(j * HEAD_DIM, (j + 1) * HEAD_DIM)
                o = part[j] + jnp.transpose(gt[hg + j:hg + j + 1] * osel_t[:, cs])
                o_ref[:, hs] = (o * out_gate[:, hs]).astype(o_ref.dtype)


def _attention_constants(seq):
    t = ATT_TILE
    n_cmp_pad = seq // CMP_STRIDE
    assert n_cmp_pad == t, "one tile of compressed keys per sequence"
    n_cmp = n_cmp_pad - CMP_BLOCK // CMP_STRIDE + 1
    n_sel = seq // SEL_BLOCK
    cst = np.arange(n_cmp_pad) * CMP_STRIDE
    sst = np.arange(n_sel) * SEL_BLOCK
    overlap = ((cst[:, None] < sst[None] + SEL_BLOCK) & (cst[:, None] + CMP_BLOCK > sst[None]))
    overlap[n_cmp:] = False
    ov = np.zeros((t, n_cmp_pad), np.float32)
    ov[:n_sel] = overlap.T
    keys = np.arange(seq)
    expand = np.zeros((seq, t), np.float32)
    expand[keys, keys // SEL_BLOCK] = NEG_INF
    eye = np.eye(t, dtype=np.float32)
    return (jnp.asarray(ov, BF16), jnp.asarray(expand, BF16), jnp.asarray(eye, BF16))


def _attention(z, kvc, bias_tile, bias_cmp, consts, bsz, seq):
    t = ATT_TILE
    kt = KEY_TILE
    n_q = seq // t
    n_kt = seq // kt
    gw = HEADS_PER_GROUP * HEAD_DIM
    kv0 = Z_KV // HEAD_DIM
    ov, expand, eye = consts

    def kv_spec(kind):
        return pl.BlockSpec((seq, HEAD_DIM),
                            lambda bg, i: (bg % bsz, kv0 + kind * KV_GROUPS + bg // bsz))

    def cmp_spec(kind):
        return pl.BlockSpec((None, None, None, t, HEAD_DIM),
                            lambda bg, i: (bg % bsz, kind, bg // bsz, 0, 0))

    def tok(bg, i):
        return (bg % bsz) * n_q + i

    return pl.pallas_call(
        _attn_kernel,
        grid=(bsz * KV_GROUPS, n_q),
        in_specs=[
            pl.BlockSpec((t, gw), lambda bg, i: (tok(bg, i), Z_Q // gw + bg // bsz)),
            kv_spec(KV_KINDS.index("k_s")), kv_spec(KV_KINDS.index("v_s")),
            kv_spec(KV_KINDS.index("k_w")), kv_spec(KV_KINDS.index("v_w")),
            cmp_spec(0), cmp_spec(1),
            pl.BlockSpec((t, V7X_LANES), lambda bg, i: (tok(bg, i), Z_GSEL // V7X_LANES + bg // bsz)),
            pl.BlockSpec((t, gw), lambda bg, i: (tok(bg, i), Z_BG // gw + bg // bsz)),
            pl.BlockSpec((None, N_BIAS_TILES, kt, gw), lambda bg, i: (bg // bsz, 0, 0, 0)),
            pl.BlockSpec((None, None, t, gw), lambda bg, i: (bg // bsz, i, 0, 0)),
            pl.BlockSpec((t, t), lambda bg, i: (0, 0)),
            pl.BlockSpec((seq, t), lambda bg, i: (0, 0)),
            pl.BlockSpec((t, t), lambda bg, i: (0, 0)),
        ],
        out_specs=pl.BlockSpec((t, gw), lambda bg, i: (tok(bg, i), bg // bsz)),
        out_shape=jax.ShapeDtypeStruct((bsz * seq, D_MODEL), BF16),
        scratch_shapes=[pltpu.VMEM((n_kt, HEAD_DIM, kt), BF16),
                        pltpu.VMEM((n_kt, HEAD_DIM, kt), BF16),
                        pltpu.VMEM((seq, HEAD_DIM + t), BF16),
                        pltpu.VMEM((n_kt, kt, gw), F32),
                        pltpu.VMEM((WINDOW // kt + 1, kt, gw), F32)],
        compiler_params=_params("parallel", "arbitrary"),
        name="nsa_attention",
    )(z, z, z, z, z, kvc, kvc, z, z, bias_tile, bias_cmp, ov, expand, eye)


def _merge_kernel(ya_ref, yb_ref, yc_ref, ma_ref, mb_ref, mc_ref, w_ref, o_ref):
    y = jax.nn.sigmoid(ma_ref[...].astype(F32)) * _dot(ya_ref[...], w_ref[0])
    y += jax.nn.sigmoid(mb_ref[...].astype(F32)) * _dot(yb_ref[...], w_ref[1])
    y += jax.nn.sigmoid(mc_ref[...].astype(F32)) * _dot(yc_ref[...], w_ref[2])
    o_ref[...] = y.astype(o_ref.dtype)


def _merge(ya, yb, yc, zf, w_branch):
    m, d = ya.shape
    tm, tn = min(1024, m), 512
    m0 = Z_MERGE // tn
    per = d // tn
    y_spec = pl.BlockSpec((tm, d), lambda i, j: (i, 0))
    return pl.pallas_call(
        _merge_kernel,
        grid=(m // tm, d // tn),
        in_specs=[
            y_spec, y_spec, y_spec,
            pl.BlockSpec((tm, tn), lambda i, j: (i, m0 + j)),
            pl.BlockSpec((tm, tn), lambda i, j: (i, m0 + per + j)),
            pl.BlockSpec((tm, tn), lambda i, j: (i, m0 + 2 * per + j)),
            pl.BlockSpec((3, d, tn), lambda i, j: (0, 0, j)),
        ],
        out_specs=pl.BlockSpec((tm, tn), lambda i, j: (i, j)),
        out_shape=jax.ShapeDtypeStruct((m, d), BF16),
        compiler_params=_params("parallel", "arbitrary", vmem=VMEM_LIMIT_BIG_TILES),
        name="merge",
    )(ya, yb, yc, zf, zf, zf, w_branch)


def _out_kernel(y_ref, w_ref, x_ref, mod_ref, g_ref, *rest, last):
    x = x_ref[...] + mod_ref[2:3, :] * _dot(y_ref[...], w_ref[...])
    normed = x * lax.rsqrt(jnp.mean(x * x, axis=-1, keepdims=True) + EPS) * g_ref[...]
    if last:
        (o_ref,) = rest
        o_ref[...] = normed
    else:
        next_mod_ref, x_out_ref, h_ref = rest
        nm = next_mod_ref[...]
        x_out_ref[...] = x
        h_ref[...] = (normed * (1.0 + nm[1:2]) + nm[0:1]).astype(h_ref.dtype)


def _out_proj(y, w_out, x2, mod3, g, next_mod3, seq):
    m, d = y.shape
    tm = 512
    per_seq = seq // tm
    last = next_mod3 is None
    row = pl.BlockSpec((tm, d), lambda i: (i, 0))
    mod_spec = pl.BlockSpec((None, 3, d), lambda i: (i // per_seq, 0, 0))
    in_specs = [row, pl.BlockSpec((d, d), lambda i: (0, 0)), row, mod_spec,
                pl.BlockSpec((1, d), lambda i: (0, 0))]
    args = [y, w_out, x2, mod3, g.reshape(1, d)]
    if last:
        out_specs = row
        out_shape = jax.ShapeDtypeStruct((m, d), F32)
    else:
        in_specs.append(mod_spec)
        args.append(next_mod3)
        out_specs = [row, row]
        out_shape = [jax.ShapeDtypeStruct((m, d), F32), jax.ShapeDtypeStruct((m, d), BF16)]
    return pl.pallas_call(
        functools.partial(_out_kernel, last=last),
        grid=(m // tm,),
        in_specs=in_specs,
        out_specs=out_specs,
        out_shape=out_shape,
        compiler_params=_params("parallel"),
        name="out_proj",
    )(*args)


def _prep_weights(w_in):
    depth, d, n_in = w_in.shape
    kvw = KV_GROUPS * HEAD_DIM
    wt = jnp.swapaxes(w_in, 1, 2)
    sizes = (d, d, d, d, 6 * kvw, 3 * N_HEADS, d, d, d, 3 * d)
    offs = np.concatenate([[0], np.cumsum(sizes)])
    a_end, q0, kv0, gsel0, tail0 = offs[3], offs[3], offs[4], offs[5], offs[6]
    gs = wt[:, gsel0:tail0].reshape(depth, 3, KV_GROUPS, HEADS_PER_GROUP, d).transpose(0, 2, 1, 3, 4)
    gs = gs.reshape(depth, KV_GROUPS, 3 * HEADS_PER_GROUP, d)
    gs = jnp.pad(gs, ((0, 0), (0, 0), (0, V7X_LANES - 3 * HEADS_PER_GROUP), (0, 0)))
    gs = gs.reshape(depth, KV_GROUPS * V7X_LANES, d)
    q_scale = HEAD_DIM ** -0.5 * LOG2_E
    w_all = jnp.concatenate([wt[:, q0:kv0] * q_scale, wt[:, tail0:], wt[:, kv0:gsel0], gs,
                             wt[:, :a_end]], axis=1)
    assert w_all.shape[1] == W_COLS
    return w_all.astype(BF16)


def kernel(x, c, rel_bias, norm_g, w_ada, b_ada, w_in, a_ln_g, a_ln_b, a_w_s, a_b_s, b_w_cmp1,
           b_w_cmp2, b_pos_cmp, c_w_grp, c_scale, w_branch, w_out, final_g):
    bsz, seq, d = x.shape
    depth = w_in.shape[0]
    assert d == D_MODEL and seq % 512 == 0
    x2 = x.reshape(bsz * seq, d)
    mod = _ada(c, w_ada, b_ada)
    bias_tile, bias_cmp = _bias_tables(rel_bias, seq)
    consts = _attention_constants(seq)
    mod3 = [mod[l].reshape(bsz, 3, d) for l in range(depth)]
    h = _prenorm(x2, norm_g[0], mod3[0], seq)
    w_all = _prep_weights(w_in)
    for l in range(depth):
        z = _proj(h, w_all, l, Z_COLS, BF16)
        ya = _proj_sgu(h, w_all, l, a_ln_g[l], a_ln_b[l], a_w_s[l], a_b_s[l])
        yc = _pool(z, c_w_grp[l], c_scale[l], seq)
        kvc = _compress(z, b_w_cmp1[l], b_w_cmp2[l], b_pos_cmp[l], bsz, seq)
        yb = _attention(z, kvc, bias_tile, bias_cmp, consts, bsz, seq)
        y = _merge(ya, yb, yc, z, w_branch[l].astype(BF16))
        if l + 1 < depth:
            x2, h = _out_proj(y, w_out[l].astype(BF16), x2, mod3[l], norm_g[l + 1], mod3[l + 1], seq)
        else:
            out = _out_proj(y, w_out[l].astype(BF16), x2, mod3[l], final_g, None, seq)
    return out.reshape(bsz, seq, d)
```

```python
import functools
import math

import numpy as np
import jax
import jax.numpy as jnp
from jax import lax
from jax.experimental import pallas as pl
from jax.experimental.pallas import tpu as pltpu

D_MODEL = 2048
A_GROUPS = 8
A_CHUNK = 128
HEAD_DIM = 128
N_HEADS = 16
KV_GROUPS = 4
HEADS_PER_GROUP = N_HEADS // KV_GROUPS
CMP_BLOCK = 32
CMP_STRIDE = 16
SEL_BLOCK = 64
N_SELECT = 16
WINDOW = 512
FORCE_BONUS = 1e4
POOL_WINDOWS = (2, 4, 8, 16)
C_GROUP = D_MODEL // len(POOL_WINDOWS)
REL_BUCKETS = 32
REL_MAX_DIST = 128
EPS = 1e-6
NEG_INF = -1e30

V7X_LANES = 128
V7X_VMEM_BYTES = 64 * 1024 * 1024
VMEM_LIMIT = 48 * 1024 * 1024
VMEM_LIMIT_BIG_TILES = 56 * 1024 * 1024

ATT_TILE = 128
KEY_TILE = 256
LOG2_E = math.log2(math.e)
POOL_HALO = 16
POOL_PAD = 32
BF16 = jnp.bfloat16
F32 = jnp.float32

Z_Q = 0
Z_BG = 2048
Z_CX, Z_CG = 4096, 6144
Z_MERGE = 8192
Z_KV = 14336
Z_GSEL = 17408
Z_COLS = 17920
W_A = Z_COLS
W_COLS = W_A + 3 * D_MODEL
KV_KINDS = ("k_c", "v_c", "k_s", "v_s", "k_w", "v_w")


def _dot(a, b):
    return jnp.dot(a, b, preferred_element_type=F32)


def _dot_nt(a, b):
    return lax.dot_general(a, b, (((1,), (1,)), ((), ())), preferred_element_type=F32)


def _params(*sem, vmem=VMEM_LIMIT):
    return pltpu.CompilerParams(dimension_semantics=sem, vmem_limit_bytes=vmem)


def _ada_kernel(c_ref, w_ref, b_ref, o_ref):
    a = jax.nn.silu(c_ref[...]).astype(BF16)
    o_ref[...] = _dot(a, w_ref[...].astype(BF16)) + b_ref[...]


def _ada(c, w_ada, b_ada):
    depth, d, n = w_ada.shape
    bsz = c.shape[0]
    tn = 1024
    return pl.pallas_call(
        _ada_kernel,
        grid=(depth, n // tn),
        in_specs=[
            pl.BlockSpec((bsz, d), lambda l, j: (0, 0)),
            pl.BlockSpec((None, d, tn), lambda l, j: (l, 0, j)),
            pl.BlockSpec((None, 1, tn), lambda l, j: (l, 0, j)),
        ],
        out_specs=pl.BlockSpec((None, bsz, tn), lambda l, j: (l, 0, j)),
        out_shape=jax.ShapeDtypeStruct((depth, bsz, n), F32),
        compiler_params=_params("parallel", "parallel"),
        name="ada_mod",
    )(c, w_ada, b_ada.reshape(depth, 1, n))


def _prenorm_kernel(x_ref, g_ref, mod_ref, o_ref):
    x = x_ref[...]
    y = x * lax.rsqrt(jnp.mean(x * x, axis=-1, keepdims=True) + EPS) * g_ref[...]
    m = mod_ref[...]
    o_ref[...] = (y * (1.0 + m[1:2]) + m[0:1]).astype(o_ref.dtype)


def _prenorm(x2, g, mod3, seq):
    m, d = x2.shape
    tm = 512
    per_seq = seq // tm
    return pl.pallas_call(
        _prenorm_kernel,
        grid=(m // tm,),
        in_specs=[
            pl.BlockSpec((tm, d), lambda i: (i, 0)),
            pl.BlockSpec((1, d), lambda i: (0, 0)),
            pl.BlockSpec((None, 3, d), lambda i: (i // per_seq, 0, 0)),
        ],
        out_specs=pl.BlockSpec((tm, d), lambda i: (i, 0)),
        out_shape=jax.ShapeDtypeStruct((m, d), BF16),
        compiler_params=_params("parallel"),
        name="prenorm",
    )(x2, g.reshape(1, d), mod3)


def _proj_kernel(a_ref, w_ref, o_ref):
    o_ref[...] = _dot_nt(a_ref[...], w_ref[...]).astype(o_ref.dtype)


def _proj(a, w_all, layer, n, out_dtype, tm=1024, tn=2560):
    m, k = a.shape
    tm = min(tm, m)
    return pl.pallas_call(
        _proj_kernel,
        grid=(m // tm, n // tn),
        in_specs=[
            pl.BlockSpec((tm, k), lambda i, j: (i, 0)),
            pl.BlockSpec((None, tn, k), lambda i, j: (layer, j, 0)),
        ],
        out_specs=pl.BlockSpec((tm, tn), lambda i, j: (i, j)),
        out_shape=jax.ShapeDtypeStruct((m, n), out_dtype),
        compiler_params=_params("parallel", "arbitrary"),
        name="in_proj",
    )(a, w_all)


def _proj_sgu_kernel(h_ref, wu_ref, wv_ref, wg_ref, lng_ref, lnb_ref, ws_ref, bst_ref, o_ref,
                     v_buf, p_buf):
    j = pl.program_id(1)
    n_j, tm, tn = v_buf.shape
    cg = D_MODEL // A_GROUPS
    h = h_ref[...]
    v_buf[j] = jax.nn.gelu(_dot_nt(h, wv_ref[...]))
    p_buf[j] = (jax.nn.gelu(_dot_nt(h, wu_ref[...]))
                * jax.nn.silu(_dot_nt(h, wg_ref[...]))).astype(p_buf.dtype)

    @pl.when(j == n_j - 1)
    def _():
        total = jnp.sum(v_buf[0], axis=-1, keepdims=True)
        for jj in range(1, n_j):
            total = total + jnp.sum(v_buf[jj], axis=-1, keepdims=True)
        mu = total * (1.0 / D_MODEL)
        sq = jnp.zeros_like(mu)
        for jj in range(n_j):
            dv = v_buf[jj] - mu
            sq = sq + jnp.sum(dv * dv, axis=-1, keepdims=True)
        rstd = lax.rsqrt(sq * (1.0 / D_MODEL) + EPS)
        row = lax.broadcasted_iota(jnp.int32, (A_CHUNK, A_CHUNK), 0)
        col = lax.broadcasted_iota(jnp.int32, (A_CHUNK, A_CHUNK), 1)
        tril = row >= col
        bst = bst_ref[...]
        for g in range(A_GROUPS):
            ws = jnp.where(tril, ws_ref[g], 0.0).astype(BF16)
            jj, off = divmod(g * cg, tn)
            cs = slice(g * cg, (g + 1) * cg)
            for c in range(tm // A_CHUNK):
                rs = slice(c * A_CHUNK, (c + 1) * A_CHUNK)
                vn = ((v_buf[jj, rs, off:off + cg] - mu[rs]) * rstd[rs] * lng_ref[:, cs]
                      + lnb_ref[:, cs]).astype(BF16)
                mixed = _dot(ws, vn) + bst[:, g:g + 1]
                o_ref[rs, cs] = (p_buf[jj, rs, off:off + cg] * mixed).astype(o_ref.dtype)


def _proj_sgu(h, w_all, layer, ln_g, ln_b, w_s, b_s):
    m, d = h.shape
    tm, tn = min(1024, m), 512
    per = d // tn
    a0 = W_A // tn
    return pl.pallas_call(
        _proj_sgu_kernel,
        grid=(m // tm, per),
        in_specs=[
            pl.BlockSpec((tm, d), lambda i, j: (i, 0)),
            pl.BlockSpec((None, tn, d), lambda i, j: (layer, a0 + j, 0)),
            pl.BlockSpec((None, tn, d), lambda i, j: (layer, a0 + per + j, 0)),
            pl.BlockSpec((None, tn, d), lambda i, j: (layer, a0 + 2 * per + j, 0)),
            pl.BlockSpec((1, d), lambda i, j: (0, 0)),
            pl.BlockSpec((1, d), lambda i, j: (0, 0)),
            pl.BlockSpec((A_GROUPS, A_CHUNK, A_CHUNK), lambda i, j: (0, 0, 0)),
            pl.BlockSpec((A_CHUNK, A_GROUPS), lambda i, j: (0, 0)),
        ],
        out_specs=pl.BlockSpec((tm, d), lambda i, j: (i, 0)),
        out_shape=jax.ShapeDtypeStruct((m, d), BF16),
        scratch_shapes=[pltpu.VMEM((per, tm, tn), F32), pltpu.VMEM((per, tm, tn), BF16)],
        compiler_params=_params("parallel", "arbitrary", vmem=VMEM_LIMIT_BIG_TILES),
        name="proj_sgu",
    )(h, w_all, w_all, w_all, ln_g.reshape(1, d), ln_b.reshape(1, d), w_s, b_s.T)


def _pool_kernel(x_ref, halo_ref, gate_ref, wg_ref, ls_ref, o_ref, ext_ref, lvl_ref, *,
                 tiles_per_seq):
    tm = x_ref.shape[0]
    i = pl.program_id(0)
    pos0 = (i % tiles_per_seq) * tm
    halo = halo_ref[...].astype(F32)
    ext_ref[0:POOL_PAD - POOL_HALO, :] = jnp.zeros((POOL_PAD - POOL_HALO, ext_ref.shape[1]), F32)
    ext_ref[POOL_PAD - POOL_HALO:POOL_PAD, :] = jnp.where(pos0 == 0, jnp.zeros_like(halo), halo)
    ext_ref[POOL_PAD:POOL_PAD + tm, :] = x_ref[...].astype(F32)
    tpos = pos0 + lax.broadcasted_iota(jnp.int32, (tm, 1), 0)
    end = POOL_PAD + tm
    for gi, w in enumerate(POOL_WINDOWS):
        cs = slice(gi * C_GROUP, (gi + 1) * C_GROUP)
        x = ext_ref[POOL_PAD:end, cs]
        levels = w.bit_length() - 1
        src = ext_ref
        for k in range(1, levels + 1):
            shift = 1 << (k - 1)
            lo = POOL_PAD - 8 * (levels - k)
            s = src[lo:end, cs] + src[lo - shift:end - shift, cs]
            if k < levels:
                lvl_ref[k - 1, lo:end, cs] = s
                src = lvl_ref.at[k - 1]
        cnt = jnp.minimum(tpos + 1, w).astype(F32)
        y = (s / cnt - x).astype(BF16)
        yg = (_dot(y, wg_ref[gi].astype(BF16)) * ls_ref[:, cs]
              * jax.nn.silu(gate_ref[:, cs].astype(F32)))
        o_ref[:, cs] = yg.astype(o_ref.dtype)


def _pool(zf, w_grp, ls, seq):
    m = zf.shape[0]
    tm = 512
    d = D_MODEL
    halo_blocks = tm // POOL_HALO
    return pl.pallas_call(
        functools.partial(_pool_kernel, tiles_per_seq=seq // tm),
        grid=(m // tm,),
        in_specs=[
            pl.BlockSpec((tm, d), lambda i: (i, Z_CX // d)),
            pl.BlockSpec((POOL_HALO, d), lambda i: (jnp.maximum(i * halo_blocks - 1, 0), Z_CX // d)),
            pl.BlockSpec((tm, d), lambda i: (i, Z_CG // d)),
            pl.BlockSpec((len(POOL_WINDOWS), C_GROUP, C_GROUP), lambda i: (0, 0, 0)),
            pl.BlockSpec((1, d), lambda i: (0, 0)),
        ],
        out_specs=pl.BlockSpec((tm, d), lambda i: (i, 0)),
        out_shape=jax.ShapeDtypeStruct((m, d), BF16),
        scratch_shapes=[pltpu.VMEM((POOL_PAD + tm, d), F32),
                        pltpu.VMEM((max(POOL_WINDOWS).bit_length() - 2, POOL_PAD + tm, d), F32)],
        compiler_params=_params("parallel"),
        name="pool",
    )(zf, zf, zf, w_grp, ls.reshape(1, d))


def _compress_kernel(x_ref, w1_ref, w2_ref, pos_ref, o_ref, xf_ref):
    n_chunks = x_ref.shape[0] // CMP_STRIDE
    xf_ref[...] = x_ref[...].astype(F32)
    pos = pos_ref[...]
    first = jnp.zeros((n_chunks, HEAD_DIM), F32)
    second = jnp.zeros((n_chunks, HEAD_DIM), F32)
    for l in range(CMP_STRIDE):
        xl = xf_ref[pl.ds(l, n_chunks, stride=CMP_STRIDE), :]
        first += _dot((xl + pos[l:l + 1]).astype(BF16), w1_ref[l].astype(BF16))
        second += _dot((xl + pos[CMP_STRIDE + l:CMP_STRIDE + l + 1]).astype(BF16),
                       w1_ref[CMP_STRIDE + l].astype(BF16))
    hdn = jax.nn.gelu(first + pltpu.roll(second, n_chunks - 1, axis=0))
    o_ref[...] = _dot(hdn.astype(BF16), w2_ref[...].astype(BF16)).astype(o_ref.dtype)


def _compress(z, w1, w2, pos, bsz, seq):
    n_chunks = seq // CMP_STRIDE
    kv0 = Z_KV // HEAD_DIM
    return pl.pallas_call(
        _compress_kernel,
        grid=(bsz, 2, KV_GROUPS),
        in_specs=[
            pl.BlockSpec((seq, HEAD_DIM), lambda b, t, g: (b, kv0 + t * KV_GROUPS + g)),
            pl.BlockSpec((None, CMP_BLOCK, HEAD_DIM, HEAD_DIM), lambda b, t, g: (t, 0, 0, 0)),
            pl.BlockSpec((None, HEAD_DIM, HEAD_DIM), lambda b, t, g: (t, 0, 0)),
            pl.BlockSpec((None, CMP_BLOCK, HEAD_DIM), lambda b, t, g: (t, 0, 0)),
        ],
        out_specs=pl.BlockSpec((None, None, None, n_chunks, HEAD_DIM), lambda b, t, g: (b, t, g, 0, 0)),
        out_shape=jax.ShapeDtypeStruct((bsz, 2, KV_GROUPS, n_chunks, HEAD_DIM), BF16),
        scratch_shapes=[pltpu.VMEM((seq, HEAD_DIM), F32)],
        compiler_params=_params("parallel", "parallel", "parallel"),
        name="compress",
    )(z, w1, w2, pos)


def _bucket_of_distance(n):
    n = np.asarray(n, np.int64)
    max_exact = REL_BUCKETS // 2
    nf = np.maximum(n, 1).astype(np.float64)
    large = max_exact + (np.log(nf / max_exact) / math.log(REL_MAX_DIST / max_exact)
                         * (REL_BUCKETS - max_exact)).astype(np.int64)
    large = np.minimum(large, REL_BUCKETS - 1)
    return np.where(n < max_exact, n, large).astype(np.int32)


def _bias_kernel(tab_ref, idx_tile_ref, idx_cmp_ref, tile_ref, cmp_ref):
    h = pl.program_id(0) * HEADS_PER_GROUP + pl.program_id(1)
    far = tab_ref[REL_BUCKETS - 1, h]
    for idx_ref, out_ref, shift in ((idx_tile_ref, tile_ref, far), (idx_cmp_ref, cmp_ref, 0.0)):
        idx = idx_ref[...]
        acc = jnp.full(idx.shape, NEG_INF, F32)
        for b in range(REL_BUCKETS):
            acc = jnp.where(idx == b, (tab_ref[b, h] - shift) * LOG2_E, acc)
        out_ref[...] = acc


N_BIAS_TILES = 7
MASKED_TILE = N_BIAS_TILES - 1


def _bias_tables(rel_bias, seq):
    t = ATT_TILE
    kt = KEY_TILE
    n_q = seq // t
    r = np.arange(t)[None, :]
    masked = REL_BUCKETS

    def tile(offset, windowed):
        d = offset + r - np.arange(kt)[:, None]
        ok = (d >= 0) & ((d < WINDOW) | (not windowed))
        return np.where(ok, _bucket_of_distance(np.maximum(d, 0)), masked)

    idx_tile = np.stack([tile(0, False), tile(t, False), tile(2 * t, False), tile(3 * t, False),
                         tile(4 * t, True), tile(5 * t, True),
                         np.full((kt, t), masked)]).astype(np.int32)
    assert idx_tile.shape[0] == N_BIAS_TILES and (idx_tile[3] == REL_BUCKETS - 1).all()
    assert (tile(3 * t, True) == idx_tile[3]).all()
    cmp_end = np.arange(t)[:, None] * CMP_STRIDE + CMP_BLOCK - 1
    idx_cmp = np.stack([_bucket_of_distance(np.maximum(i * t + r - cmp_end, 0))
                        for i in range(n_q)]).astype(np.int32)
    return pl.pallas_call(
        _bias_kernel,
        grid=(KV_GROUPS, HEADS_PER_GROUP),
        in_specs=[
            pl.BlockSpec(memory_space=pltpu.SMEM),
            pl.BlockSpec((N_BIAS_TILES, kt, t), lambda g, j: (0, 0, 0)),
            pl.BlockSpec((n_q, t, t), lambda g, j: (0, 0, 0)),
        ],
        out_specs=[
            pl.BlockSpec((None, N_BIAS_TILES, kt, t), lambda g, j: (g, 0, 0, j)),
            pl.BlockSpec((None, n_q, t, t), lambda g, j: (g, 0, 0, j)),
        ],
        out_shape=[
            jax.ShapeDtypeStruct((KV_GROUPS, N_BIAS_TILES, kt, HEADS_PER_GROUP * t), F32),
            jax.ShapeDtypeStruct((KV_GROUPS, n_q, t, HEADS_PER_GROUP * t), F32),
        ],
        compiler_params=_params("parallel", "parallel"),
        name="bias_tables",
    )(rel_bias, jnp.asarray(idx_tile), jnp.asarray(idx_cmp))


def _attn_kernel(q_ref, ks_ref, vs_ref, kw_ref, vw_ref, kc_ref, vc_ref, gsel_ref, bgate_ref,
                 btile_ref, bcmp_ref, ov_ref, expand_ref, eye_ref, o_ref,
                 vst_ref, vwt_ref, kaug_ref, s_ref, w_ref):
    t = ATT_TILE
    kt = KEY_TILE
    hg = HEADS_PER_GROUP
    cols = hg * t
    n_kt = s_ref.shape[0]
    n_q = n_kt * kt // t
    i = pl.program_id(1)
    last_tile = lax.shift_right_logical(i, 1)
    eye = eye_ref[...]

    @pl.when(i == 0)
    def _():
        kaug_ref[:, 0:HEAD_DIM] = ks_ref[...]
        kaug_ref[:, HEAD_DIM:] = expand_ref[...]
        for jt in range(n_kt):
            ks = slice(jt * kt, (jt + 1) * kt)
            vst_ref[jt] = _dot_nt(eye, vs_ref[ks, :]).astype(BF16)
            vwt_ref[jt] = _dot_nt(eye, vw_ref[ks, :]).astype(BF16)

    qb = q_ref[...]
    qs = jnp.concatenate([qb[:, j * HEAD_DIM:(j + 1) * HEAD_DIM] for j in range(hg)], axis=0)

    def bias_index(delta):
        return jnp.where(delta < 0, MASKED_TILE, delta)

    win_tiles = WINDOW // kt + 1
    win_first = jnp.maximum(last_tile - (win_tiles - 1), 0)

    def win_score_tile(js):
        jt = win_first + js
        k = kw_ref[pl.ds(pl.multiple_of(jt * kt, kt), kt), :]
        st = _dot_nt(k, qs) + btile_ref[bias_index(i - 2 * jt)]
        w_ref[js] = st
        return st

    n_idx = lax.broadcasted_iota(jnp.int32, (t, cols), 0)
    tq = i * t + (lax.broadcasted_iota(jnp.int32, (t, cols), 1) & (t - 1))
    valid = tq >= n_idx * CMP_STRIDE + (CMP_BLOCK - 1)
    s = jnp.where(valid, _dot_nt(kc_ref[...], qs) + bcmp_ref[...], NEG_INF)
    win_mx = win_score_tile(0)
    e = jnp.where(valid, jnp.exp2(s - jnp.max(s, axis=0, keepdims=True)), 0.0)
    l = jnp.sum(e, axis=0, keepdims=True)
    p_c = e * (1.0 / jnp.where(l > 0.0, l, 1.0))
    vct = _dot_nt(eye, vc_ref[...]).astype(BF16)
    ocmp_t = _dot(vct, p_c.astype(BF16))

    psum = p_c[:, 0:t]
    for j in range(1, hg):
        psum = psum + p_c[:, j * t:(j + 1) * t]
    p_hi = psum.astype(BF16)
    p_lo = (psum - p_hi.astype(F32)).astype(BF16)
    ov = ov_ref[...]
    n_sel = n_kt * kt // SEL_BLOCK
    imp = (_dot(ov, p_hi) + _dot(ov, p_lo))[0:n_sel]
    win_mx = jnp.maximum(win_mx, win_score_tile(1))
    blk = lax.broadcasted_iota(jnp.int32, (n_sel, t), 0)
    tqb = i * t + lax.broadcasted_iota(jnp.int32, (n_sel, t), 1)
    cur = lax.shift_right_logical(tqb, int(math.log2(SEL_BLOCK)))
    forced = (blk == 0) | (blk == cur) | (blk == cur - 1)
    future = blk * SEL_BLOCK > tqb
    imp = jnp.where(future, -1.0, imp + jnp.where(forced, FORCE_BONUS, 0.0))
    rank = jnp.zeros((n_sel, t), F32)
    for sp in range(n_sel):
        other = imp[sp:sp + 1, :]
        beats = (other > imp) | ((other == imp) & (blk > sp))
        rank = rank + jnp.where(beats, 1.0, 0.0)
    unsel = jnp.where(rank < float(min(N_SELECT, n_sel)), 0.0, 1.0)
    unsel_pad = jnp.concatenate([unsel, jnp.zeros((t - n_sel, t), F32)], axis=0).astype(BF16)
    unsel_qb = _dot_nt(eye, unsel_pad).astype(BF16)
    for js in range(2, win_tiles):
        win_mx = jnp.maximum(win_mx, win_score_tile(js))
    m_win = jnp.max(win_mx, axis=0, keepdims=True)
    q_aug = jnp.concatenate([qs, jnp.concatenate([unsel_qb] * hg, axis=0)], axis=1)

    def value_tile(carry, m, vt, sc_ref, js):
        lsum, acc = carry
        p = jnp.exp2(sc_ref[js] - m)
        return lsum + p, acc + _dot(vt, p.astype(BF16))

    def normalised(carry):
        lsum, acc = carry
        return acc * (1.0 / jnp.sum(lsum, axis=0, keepdims=True))

    zero_carry = (jnp.zeros((kt, cols), F32), jnp.zeros((HEAD_DIM, cols), F32))
    gt = jnp.transpose(jax.nn.sigmoid(gsel_ref[...].astype(F32)))
    out_gate = jax.nn.silu(bgate_ref[...].astype(F32))

    q_per_variant = kt // t
    far_offset = 3
    for variant in range(n_q // q_per_variant):
        n_sub = variant + 1
        n_far = max((q_per_variant * variant - far_offset) // 2 + 1, 0)
        first = max(variant - (win_tiles - 1), 0)

        def sel_bias(js, n_far=n_far):
            if js < n_far:
                return None
            return btile_ref[bias_index(jnp.minimum(i - 2 * js, far_offset))]

        @pl.when(last_tile == variant)
        def _(n_sub=n_sub, sel_bias=sel_bias, first=first):
            mx = jnp.full((kt, cols), NEG_INF, F32)
            for js in range(n_sub):
                st = _dot_nt(kaug_ref[js * kt:(js + 1) * kt, :], q_aug)
                bias = sel_bias(js)
                if bias is not None:
                    st = st + bias
                s_ref[js] = st
                mx = jnp.maximum(mx, st)
            win = zero_carry
            for js in range(win_tiles):
                win = value_tile(win, m_win, vwt_ref[first + js], w_ref, js)
            owin_t = normalised(win)
            m_sel = jnp.max(mx, axis=0, keepdims=True)
            sel = zero_carry
            part = []
            for j in range(hg):
                cs = slice(j * t, (j + 1) * t)
                part.append(jnp.transpose(gt[j:j + 1] * ocmp_t[:, cs]
                                          + gt[2 * hg + j:2 * hg + j + 1] * owin_t[:, cs]))
            for js in range(n_sub):
                sel = value_tile(sel, m_sel, vst_ref[js], s_ref, js)
            osel_t = normalised(sel)
            for j in range(hg):
                cs = slice(j * t, (j + 1) * t)
                hs = slice(j * HEAD_DIM, (j + 1) * HEAD_DIM)
                o = part[j] + jnp.transpose(gt[hg + j:hg + j + 1] * osel_t[:, cs])
                o_ref[:, hs] = (o * out_gate[:, hs]).astype(o_ref.dtype)


def _attention_constants(seq):
    t = ATT_TILE
    n_cmp_pad = seq // CMP_STRIDE
    assert n_cmp_pad == t, "one tile of compressed keys per sequence"
    n_cmp = n_cmp_pad - CMP_BLOCK // CMP_STRIDE + 1
    n_sel = seq // SEL_BLOCK
    cst = np.arange(n_cmp_pad) * CMP_STRIDE
    sst = np.arange(n_sel) * SEL_BLOCK
    overlap = ((cst[:, None] < sst[None] + SEL_BLOCK) & (cst[:, None] + CMP_BLOCK > sst[None]))
    overlap[n_cmp:] = False
    ov = np.zeros((t, n_cmp_pad), np.float32)
    ov[:n_sel] = overlap.T
    keys = np.arange(seq)
    expand = np.zeros((seq, t), np.float32)
    expand[keys, keys // SEL_BLOCK] = NEG_INF
    eye = np.eye(t, dtype=np.float32)
    return (jnp.asarray(ov, BF16), jnp.asarray(expand, BF16), jnp.asarray(eye, BF16))


def _attention(z, kvc, bias_tile, bias_cmp, consts, bsz, seq):
    t = ATT_TILE
    kt = KEY_TILE
    n_q = seq // t
    n_kt = seq // kt
    gw = HEADS_PER_GROUP * HEAD_DIM
    kv0 = Z_KV // HEAD_DIM
    ov, expand, eye = consts

    def kv_spec(kind):
        return pl.BlockSpec((seq, HEAD_DIM),
                            lambda bg, i: (bg // KV_GROUPS, kv0 + kind * KV_GROUPS + bg % KV_GROUPS))

    def cmp_spec(kind):
        return pl.BlockSpec((None, None, None, t, HEAD_DIM),
                            lambda bg, i: (bg // KV_GROUPS, kind, bg % KV_GROUPS, 0, 0))

    def tok(bg, i):
        return (bg // KV_GROUPS) * n_q + i

    return pl.pallas_call(
        _attn_kernel,
        grid=(bsz * KV_GROUPS, n_q),
        in_specs=[
            pl.BlockSpec((t, gw), lambda bg, i: (tok(bg, i), Z_Q // gw + bg % KV_GROUPS)),
            kv_spec(KV_KINDS.index("k_s")), kv_spec(KV_KINDS.index("v_s")),
            kv_spec(KV_KINDS.index("k_w")), kv_spec(KV_KINDS.index("v_w")),
            cmp_spec(0), cmp_spec(1),
            pl.BlockSpec((t, V7X_LANES), lambda bg, i: (tok(bg, i), Z_GSEL // V7X_LANES + bg % KV_GROUPS)),
            pl.BlockSpec((t, gw), lambda bg, i: (tok(bg, i), Z_BG // gw + bg % KV_GROUPS)),
            pl.BlockSpec((None, N_BIAS_TILES, kt, gw), lambda bg, i: (bg % KV_GROUPS, 0, 0, 0)),
            pl.BlockSpec((None, None, t, gw), lambda bg, i: (bg % KV_GROUPS, i, 0, 0)),
            pl.BlockSpec((t, t), lambda bg, i: (0, 0)),
            pl.BlockSpec((seq, t), lambda bg, i: (0, 0)),
            pl.BlockSpec((t, t), lambda bg, i: (0, 0)),
        ],
        out_specs=pl.BlockSpec((t, gw), lambda bg, i: (tok(bg, i), bg % KV_GROUPS)),
        out_shape=jax.ShapeDtypeStruct((bsz * seq, D_MODEL), BF16),
        scratch_shapes=[pltpu.VMEM((n_kt, HEAD_DIM, kt), BF16),
                        pltpu.VMEM((n_kt, HEAD_DIM, kt), BF16),
                        pltpu.VMEM((seq, HEAD_DIM + t), BF16),
                        pltpu.VMEM((n_kt, kt, gw), F32),
                        pltpu.VMEM((WINDOW // kt + 1, kt, gw), F32)],
        compiler_params=_params("parallel", "arbitrary"),
        name="nsa_attention",
    )(z, z, z, z, z, kvc, kvc, z, z, bias_tile, bias_cmp, ov, expand, eye)


def _merge_kernel(ya_ref, yb_ref, yc_ref, ma_ref, mb_ref, mc_ref, w_ref, o_ref):
    y = jax.nn.sigmoid(ma_ref[...].astype(F32)) * _dot(ya_ref[...], w_ref[0])
    y += jax.nn.sigmoid(mb_ref[...].astype(F32)) * _dot(yb_ref[...], w_ref[1])
    y += jax.nn.sigmoid(mc_ref[...].astype(F32)) * _dot(yc_ref[...], w_ref[2])
    o_ref[...] = y.astype(o_ref.dtype)


def _merge(ya, yb, yc, zf, w_branch):
    m, d = ya.shape
    tm, tn = min(1024, m), 512
    m0 = Z_MERGE // tn
    per = d // tn
    y_spec = pl.BlockSpec((tm, d), lambda i, j: (i, 0))
    return pl.pallas_call(
        _merge_kernel,
        grid=(m // tm, d // tn),
        in_specs=[
            y_spec, y_spec, y_spec,
            pl.BlockSpec((tm, tn), lambda i, j: (i, m0 + j)),
            pl.BlockSpec((tm, tn), lambda i, j: (i, m0 + per + j)),
            pl.BlockSpec((tm, tn), lambda i, j: (i, m0 + 2 * per + j)),
            pl.BlockSpec((3, d, tn), lambda i, j: (0, 0, j)),
        ],
        out_specs=pl.BlockSpec((tm, tn), lambda i, j: (i, j)),
        out_shape=jax.ShapeDtypeStruct((m, d), BF16),
        compiler_params=_params("parallel", "arbitrary", vmem=VMEM_LIMIT_BIG_TILES),
        name="merge",
    )(ya, yb, yc, zf, zf, zf, w_branch)


def _out_kernel(y_ref, w_ref, x_ref, mod_ref, g_ref, *rest, last):
    x = x_ref[...] + mod_ref[2:3, :] * _dot(y_ref[...], w_ref[...])
    normed = x * lax.rsqrt(jnp.mean(x * x, axis=-1, keepdims=True) + EPS) * g_ref[...]
    if last:
        (o_ref,) = rest
        o_ref[...] = normed
    else:
        next_mod_ref, x_out_ref, h_ref = rest
        nm = next_mod_ref[...]
        x_out_ref[...] = x
        h_ref[...] = (normed * (1.0 + nm[1:2]) + nm[0:1]).astype(h_ref.dtype)


def _out_proj(y, w_out, x2, mod3, g, next_mod3, seq):
    m, d = y.shape
    tm = 512
    per_seq = seq // tm
    last = next_mod3 is None
    row = pl.BlockSpec((tm, d), lambda i: (i, 0))
    mod_spec = pl.BlockSpec((None, 3, d), lambda i: (i // per_seq, 0, 0))
    in_specs = [row, pl.BlockSpec((d, d), lambda i: (0, 0)), row, mod_spec,
                pl.BlockSpec((1, d), lambda i: (0, 0))]
    args = [y, w_out, x2, mod3, g.reshape(1, d)]
    if last:
        out_specs = row
        out_shape = jax.ShapeDtypeStruct((m, d), F32)
    else:
        in_specs.append(mod_spec)
        args.append(next_mod3)
        out_specs = [row, row]
        out_shape = [jax.ShapeDtypeStruct((m, d), F32), jax.ShapeDtypeStruct((m, d), BF16)]
    return pl.pallas_call(
        functools.partial(_out_kernel, last=last),
        grid=(m // tm,),
        in_specs=in_specs,
        out_specs=out_specs,
        out_shape=out_shape,
        compiler_params=_params("parallel"),
        name="out_proj",
    )(*args)


def _prep_weights(w_in):
    depth, d, n_in = w_in.shape
    kvw = KV_GROUPS * HEAD_DIM
    wt = jnp.swapaxes(w_in, 1, 2)
    sizes = (d, d, d, d, 6 * kvw, 3 * N_HEADS, d, d, d, 3 * d)
    offs = np.concatenate([[0], np.cumsum(sizes)])
    a_end, q0, kv0, gsel0, tail0 = offs[3], offs[3], offs[4], offs[5], offs[6]
    gs = wt[:, gsel0:tail0].reshape(depth, 3, KV_GROUPS, HEADS_PER_GROUP, d).transpose(0, 2, 1, 3, 4)
    gs = gs.reshape(depth, KV_GROUPS, 3 * HEADS_PER_GROUP, d)
    gs = jnp.pad(gs, ((0, 0), (0, 0), (0, V7X_LANES - 3 * HEADS_PER_GROUP), (0, 0)))
    gs = gs.reshape(depth, KV_GROUPS * V7X_LANES, d)
    q_scale = HEAD_DIM ** -0.5 * LOG2_E
    w_all = jnp.concatenate([wt[:, q0:kv0] * q_scale, wt[:, tail0:], wt[:, kv0:gsel0], gs,
                             wt[:, :a_end]], axis=1)
    assert w_all.shape[1] == W_COLS
    return w_all.astype(BF16)


def kernel(x, c, rel_bias, norm_g, w_ada, b_ada, w_in, a_ln_g, a_ln_b, a_w_s, a_b_s, b_w_cmp1,
           b_w_cmp2, b_pos_cmp, c_w_grp, c_scale, w_branch, w_out, final_g):
    bsz, seq, d = x.shape
    depth = w_in.shape[0]
    assert d == D_MODEL and seq % 512 == 0
    x2 = x.reshape(bsz * seq, d)
    mod = _ada(c, w_ada, b_ada)
    bias_tile, bias_cmp = _bias_tables(rel_bias, seq)
    consts = _attention_constants(seq)
    mod3 = [mod[l].reshape(bsz, 3, d) for l in range(depth)]
    h = _prenorm(x2, norm_g[0], mod3[0], seq)
    w_all = _prep_weights(w_in)
    for l in range(depth):
        z = _proj(h, w_all, l, Z_COLS, BF16)
        ya = _proj_sgu(h, w_all, l, a_ln_g[l], a_ln_b[l], a_w_s[l], a_b_s[l])
        yc = _pool(z, c_w_grp[l], c_scale[l], seq)
        kvc = _compress(z, b_w_cmp1[l], b_w_cmp2[l], b_pos_cmp[l], bsz, seq)
        yb = _attention(z, kvc, bias_tile, bias_cmp, consts, bsz, seq)
        y = _merge(ya, yb, yc, z, w_branch[l].astype(BF16))
        if l + 1 < depth:
            x2, h = _out_proj(y, w_out[l].astype(BF16), x2, mod3[l], norm_g[l + 1], mod3[l + 1], seq)
        else:
            out = _out_proj(y, w_out[l].astype(BF16), x2, mod3[l], final_g, None, seq)
    return out.reshape(bsz, seq, d)
```
